```python
import jax, jax.numpy as jnp
from jax import lax
import numpy as np

D_MODEL = 2048
BATCH = 8
SEQ = 4096
DEPTH = 2

HEAD_DIM = 64
N_MIX_HEADS = D_MODEL // HEAD_DIM
A_Q_HEADS = N_MIX_HEADS // 2
A_KV_HEADS = A_Q_HEADS // 8
A_GROUP = A_Q_HEADS // A_KV_HEADS
A_WINDOW = 128
B_HEADS = N_MIX_HEADS // 4
MOBA_BLOCK = 256
MOBA_TOPK = 3
MOBA_QCHUNK = 64
C_HEADS = N_MIX_HEADS // 4
DILATED_PAIRS = ((128, 1), (512, 4), (2048, 16))
BAND = 128
A_Q_W = A_Q_HEADS * HEAD_DIM
A_KV_W = A_KV_HEADS * HEAD_DIM
B_W = B_HEADS * HEAD_DIM
C_W = C_HEADS * HEAD_DIM
IN_SPLITS = (A_Q_W, A_KV_W, A_KV_W, B_W, B_W, B_W, C_W, C_W, C_W)
IN_WIDTH = A_Q_W + 2 * A_KV_W + 3 * B_W + 3 * C_W
MIX_WIDTH = A_Q_W + B_W + C_W
MEM_LEN = 256
MEM_HEADS = 4
MEM_HEAD_DIM = 128
MEM_WIDTH = MEM_HEADS * MEM_HEAD_DIM
N_EXPERTS = 32
TOPK = 4
D_FF = 2048
SWIGLU_LIMIT = 7.0
SWIGLU_ALPHA = 1.702
MOE_BLOCK = 256
LN_EPS = 1e-5
DEEPNORM_ALPHA = (2 * DEPTH) ** 0.25
DEEPNORM_BETA = (8 * DEPTH) ** -0.25

kernel_name = 'hybrid_swa_moba_dilated_memory_moe_block'


def _alibi_slopes(n):
    return jnp.asarray(2.0 ** (-8.0 * np.arange(1, n + 1) / n), dtype=jnp.float32)


def _layer_norm(x, g, b):
    xf = x.astype(jnp.float32)
    mu = xf.mean(-1, keepdims=True)
    var = jnp.square(xf - mu).mean(-1, keepdims=True)
    y = (xf - mu) * lax.rsqrt(var + LN_EPS) * g.astype(jnp.float32) + b.astype(jnp.float32)
    return y.astype(x.dtype)


def _pad_len(t, axis, target):
    pad = [(0, 0)] * t.ndim
    pad[axis] = (0, target - t.shape[axis])
    return jnp.pad(t, pad)


def _band_keys(t, nb):
    n, h, _, d = t.shape
    tb = _pad_len(t, 2, nb * BAND).reshape(n, h, nb, BAND, d)
    prev = jnp.pad(tb[:, :, :-1], ((0, 0), (0, 0), (1, 0), (0, 0), (0, 0)))
    return jnp.concatenate([prev, tb], axis=3)


def _banded_logits(q, k, slopes, max_dist, dist_scale):
    n, hk, g, L, d = q.shape
    nb = -(-L // BAND)
    qb = _pad_len(q, 3, nb * BAND).reshape(n, hk, g, nb, BAND, d)
    s = jnp.einsum('nhgiqd,nhikd->nhgiqk', qb, _band_keys(k, nb)).astype(jnp.float32) * (d ** -0.5)
    qi = jnp.arange(BAND)[:, None]
    ki = jnp.arange(2 * BAND)[None, :]
    dist = BAND + qi - ki
    blk = jnp.arange(nb)[:, None, None]
    valid = (dist >= 0) & (dist <= max_dist) & ((blk > 0) | (ki >= BAND))
    bias = -slopes[:, :, None, None] * (dist_scale * dist).astype(jnp.float32)
    return jnp.where(valid, s + bias[:, :, None], -jnp.inf), nb


def _sliding_window_sink_attn(q, k, v, slopes, sinks):
    bsz, hk, g, L, d = q.shape
    logits, nb = _banded_logits(q, k, slopes, A_WINDOW - 1, 1)
    sink = jnp.broadcast_to(sinks.astype(jnp.float32)[None, :, :, None, None, None], logits.shape[:-1] + (1,))
    p = jax.nn.softmax(jnp.concatenate([logits, sink], axis=-1), axis=-1)[..., :-1]
    o = jnp.einsum('nhgiqk,nhikd->nhgiqd', p.astype(v.dtype), _band_keys(v, nb))
    return o.reshape(bsz, hk, g, nb * BAND, d)[:, :, :, :L]


def _to_strided(t, r):
    b, h, s, d = t.shape
    return t.reshape(b, h, s // r, r, d).transpose(0, 3, 1, 2, 4).reshape(b * r, h, s // r, d)


def _from_strided(t, b, r):
    h, L = t.shape[1], t.shape[2]
    rest = t.shape[3:]
    t = jnp.moveaxis(t.reshape((b, r, h, L) + rest), 1, 3)
    return t.reshape((b, h, L * r) + rest)


def _dilated_mixture_attn(q, k, v, slopes):
    bsz, h, S, d = q.shape
    outs, lses = [], []
    for window, r in DILATED_PAIRS:
        L = S // r
        logits, nb = _banded_logits(_to_strided(q, r)[:, :, None], _to_strided(k, r), slopes[:, None], window // r, r)
        lse = jax.nn.logsumexp(logits, axis=-1, keepdims=True)
        o = jnp.einsum('nhgiqk,nhikd->nhgiqd', jnp.exp(logits - lse).astype(v.dtype), _band_keys(_to_strided(v, r), nb))
        o = o.reshape(bsz * r, h, nb * BAND, d)[:, :, :L]
        lse = lse.reshape(bsz * r, h, nb * BAND)[:, :, :L]
        outs.append(_from_strided(o, bsz, r))
        lses.append(_from_strided(lse, bsz, r))
    w = jax.nn.softmax(jnp.stack(lses), axis=0)
    return jnp.einsum('pbhs,pbhsd->bhsd', w.astype(v.dtype), jnp.stack(outs))


def _moba_attn(q, k, v, slopes):
    bsz, h, S, d = q.shape
    nb = -(-S // MOBA_BLOCK)
    Sp = nb * MOBA_BLOCK
    q, k, v = _pad_len(q, 2, Sp), _pad_len(k, 2, Sp), _pad_len(v, 2, Sp)
    kb = k.reshape(bsz, h, nb, MOBA_BLOCK, d)
    vb = v.reshape(bsz, h, nb, MOBA_BLOCK, d)
    n_cand = max(nb, MOBA_TOPK)
    gate = _pad_len(jnp.einsum('bhsd,bhnd->bhsn', q, kb.mean(axis=3)).astype(jnp.float32), 3, n_cand)
    q_blk = jnp.arange(Sp) // MOBA_BLOCK
    past = jnp.arange(n_cand)[None, :] < q_blk[:, None]
    _, sel = lax.top_k(jnp.where(past, gate, -jnp.inf), MOBA_TOPK)
    sel_valid = sel < q_blk[:, None]
    sel = jnp.minimum(sel, nb - 1)
    nc = Sp // MOBA_QCHUNK
    n_sel = MOBA_TOPK * MOBA_BLOCK
    scale = d ** -0.5
    bi = jnp.arange(bsz)[:, None, None, None]
    hi = jnp.arange(h)[None, :, None, None]

    def chunks(t):
        return jnp.moveaxis(t.reshape((bsz, h, nc, MOBA_QCHUNK) + t.shape[3:]), 2, 0)

    def step(args):
        qc, selc, validc, c = args
        tpos = c * MOBA_QCHUNK + jnp.arange(MOBA_QCHUNK)
        j = (c * MOBA_QCHUNK) // MOBA_BLOCK
        kg = kb[bi, hi, selc]
        vg = vb[bi, hi, selc]
        s_sel = jnp.einsum('bhqd,bhqnkd->bhqnk', qc, kg).astype(jnp.float32) * scale
        kpos_sel = selc[..., None] * MOBA_BLOCK + jnp.arange(MOBA_BLOCK)
        dist_sel = (tpos[:, None, None] - kpos_sel).astype(jnp.float32)
        s_sel = jnp.where(validc[..., None], s_sel - slopes[:, None, None, None] * dist_sel, -jnp.inf)
        k_own = lax.dynamic_index_in_dim(kb, j, axis=2, keepdims=False)
        v_own = lax.dynamic_index_in_dim(vb, j, axis=2, keepdims=False)
        s_own = jnp.einsum('bhqd,bhkd->bhqk', qc, k_own).astype(jnp.float32) * scale
        dist_own = tpos[:, None] - (j * MOBA_BLOCK + jnp.arange(MOBA_BLOCK))[None, :]
        s_own = jnp.where(dist_own >= 0, s_own - slopes[:, None, None] * dist_own.astype(jnp.float32), -jnp.inf)
        p = jax.nn.softmax(jnp.concatenate([s_sel.reshape(bsz, h, MOBA_QCHUNK, n_sel), s_own], axis=-1), axis=-1).astype(v.dtype)
        return (jnp.einsum('bhqnk,bhqnkd->bhqd', p[..., :n_sel].reshape(s_sel.shape), vg)
                + jnp.einsum('bhqk,bhkd->bhqd', p[..., n_sel:], v_own))

    out = lax.map(step, (chunks(q), chunks(sel), chunks(sel_valid), jnp.arange(nc)))
    return jnp.moveaxis(out, 0, 2).reshape(bsz, h, Sp, d)[:, :, :S]


def _hybrid_mixer(h, w_in, b_in, sinks, w_out):
    bsz, S, _ = h.shape
    proj = h @ w_in + b_in
    aq, ak, av, bq, bk, bv, cq, ck, cv = jnp.split(proj, np.cumsum(IN_SPLITS)[:-1].tolist(), axis=-1)

    def heads(t, n):
        return t.reshape(bsz, S, n, HEAD_DIM).transpose(0, 2, 1, 3)

    qa = heads(aq, A_Q_HEADS).reshape(bsz, A_KV_HEADS, A_GROUP, S, HEAD_DIM)
    oa = _sliding_window_sink_attn(qa, heads(ak, A_KV_HEADS), heads(av, A_KV_HEADS),
                                   _alibi_slopes(A_Q_HEADS).reshape(A_KV_HEADS, A_GROUP),
                                   sinks.reshape(A_KV_HEADS, A_GROUP)).reshape(bsz, A_Q_HEADS, S, HEAD_DIM)
    ob = _moba_attn(heads(bq, B_HEADS), heads(bk, B_HEADS), heads(bv, B_HEADS), _alibi_slopes(B_HEADS))
    oc = _dilated_mixture_attn(heads(cq, C_HEADS), heads(ck, C_HEADS), heads(cv, C_HEADS), _alibi_slopes(C_HEADS))
    o = jnp.concatenate([oa, ob, oc], axis=1)
    return o.transpose(0, 2, 1, 3).reshape(bsz, S, MIX_WIDTH) @ w_out


def _memory_cross_attn(h, mem, w_q, w_k, w_v, w_o):
    bsz, S, _ = h.shape
    m = mem.shape[1]
    q = (h @ w_q).reshape(bsz, S, MEM_HEADS, MEM_HEAD_DIM)
    k = (mem @ w_k).reshape(bsz, m, MEM_HEADS, MEM_HEAD_DIM)
    v = (mem @ w_v).reshape(bsz, m, MEM_HEADS, MEM_HEAD_DIM)
    s = jnp.einsum('bshd,bmhd->bhsm', q, k).astype(jnp.float32) * (MEM_HEAD_DIM ** -0.5)
    p = jax.nn.softmax(s, axis=-1).astype(v.dtype)
    return jnp.einsum('bhsm,bmhd->bshd', p, v).reshape(bsz, S, MEM_WIDTH) @ w_o


def _moe_ffn(h, w_router, b_router, w_gu, b_gu, w_down, b_down):
    bsz, S, D = h.shape
    T = bsz * S
    xf = h.reshape(T, D)
    logits = xf.astype(jnp.float32) @ w_router.astype(jnp.float32) + b_router.astype(jnp.float32)
    top_val, top_idx = lax.top_k(logits, TOPK)
    gates = jax.nn.softmax(top_val, axis=-1)
    n_assign = T * TOPK
    e_flat = top_idx.reshape(-1)
    order = jnp.argsort(e_flat)
    e_sorted = e_flat[order]
    counts = jnp.bincount(e_flat, length=N_EXPERTS)
    padded = (counts + MOE_BLOCK - 1) // MOE_BLOCK * MOE_BLOCK
    start = jnp.cumsum(counts) - counts
    p_end = jnp.cumsum(padded)
    p_start = p_end - padded
    dest = p_start[e_sorted] + jnp.arange(n_assign) - start[e_sorted]
    n_rows = -(-n_assign // MOE_BLOCK) * MOE_BLOCK + N_EXPERTS * MOE_BLOCK
    n_blocks = n_rows // MOE_BLOCK
    row_tok = jnp.full((n_rows,), T, jnp.int32).at[dest].set((order // TOPK).astype(jnp.int32))
    row_gate = jnp.zeros((n_rows,), jnp.float32).at[dest].set(gates.reshape(-1)[order])
    block_expert = jnp.minimum(jnp.searchsorted(p_end, jnp.arange(n_blocks) * MOE_BLOCK, side='right'), N_EXPERTS - 1)
    x_pad = jnp.concatenate([xf, jnp.zeros((1, D), xf.dtype)], axis=0)

    def expert_block(args):
        tok, e = args
        hid = x_pad[tok] @ w_gu[e] + b_gu[e]
        gate = jnp.minimum(hid[:, :D_FF], SWIGLU_LIMIT)
        up = jnp.clip(hid[:, D_FF:], -SWIGLU_LIMIT, SWIGLU_LIMIT)
        return ((up + 1.0) * (gate * jax.nn.sigmoid(SWIGLU_ALPHA * gate))) @ w_down[e] + b_down[e]

    y = lax.map(expert_block, (row_tok.reshape(n_blocks, MOE_BLOCK), block_expert))
    y = y.reshape(n_rows, D) * row_gate[:, None].astype(y.dtype)
    return jax.ops.segment_sum(y, row_tok, num_segments=T + 1)[:T].reshape(bsz, S, D)


def setup_inputs(seed: int = 0) -> dict:
    key = jax.random.key(seed)
    ks = jax.random.split(key, 22)
    f32 = jnp.float32

    def nrm(k, shape, std):
        return jax.random.normal(k, shape, f32) * std

    offs = np.cumsum((0,) + IN_SPLITS)
    col_scale = np.ones(IN_WIDTH, np.float32)
    for i in (2, 5, 8):
        col_scale[offs[i]:offs[i + 1]] = DEEPNORM_BETA
    return {
        'x': nrm(ks[0], (BATCH, SEQ, D_MODEL), 1.0),
        'mem': nrm(ks[1], (BATCH, MEM_LEN, D_MODEL), 1.0),
        'w_in': nrm(ks[2], (DEPTH, D_MODEL, IN_WIDTH), D_MODEL ** -0.5) * jnp.asarray(col_scale),
        'b_in': nrm(ks[3], (DEPTH, IN_WIDTH), 0.02),
        'attn_sinks': nrm(ks[4], (DEPTH, A_Q_HEADS), 0.5),
        'w_out': nrm(ks[5], (DEPTH, MIX_WIDTH, D_MODEL), MIX_WIDTH ** -0.5 * DEEPNORM_BETA),
        'ln1_g': 1.0 + nrm(ks[6], (DEPTH, D_MODEL), 0.02),
        'ln1_b': nrm(ks[7], (DEPTH, D_MODEL), 0.02),
        'w_mem_q': nrm(ks[8], (DEPTH, D_MODEL, MEM_WIDTH), D_MODEL ** -0.5),
        'w_mem_k': nrm(ks[9], (DEPTH, D_MODEL, MEM_WIDTH), D_MODEL ** -0.5),
        'w_mem_v': nrm(ks[10], (DEPTH, D_MODEL, MEM_WIDTH), D_MODEL ** -0.5 * DEEPNORM_BETA),
        'w_mem_o': nrm(ks[11], (DEPTH, MEM_WIDTH, D_MODEL), MEM_WIDTH ** -0.5 * DEEPNORM_BETA),
        'ln2_g': 1.0 + nrm(ks[12], (DEPTH, D_MODEL), 0.02),
        'ln2_b': nrm(ks[13], (DEPTH, D_MODEL), 0.02),
        'w_router': nrm(ks[14], (DEPTH, D_MODEL, N_EXPERTS), D_MODEL ** -0.5),
        'b_router': nrm(ks[15], (DEPTH, N_EXPERTS), 0.01),
        'w_gate_up': nrm(ks[16], (DEPTH, N_EXPERTS, D_MODEL, 2 * D_FF), D_MODEL ** -0.5),
        'b_gate_up': nrm(ks[17], (DEPTH, N_EXPERTS, 2 * D_FF), 0.02),
        'w_down': nrm(ks[18], (DEPTH, N_EXPERTS, D_FF, D_MODEL), D_FF ** -0.5 * DEEPNORM_BETA),
        'b_down': nrm(ks[19], (DEPTH, N_EXPERTS, D_MODEL), 0.02),
        'ln3_g': 1.0 + nrm(ks[20], (DEPTH, D_MODEL), 0.02),
        'ln3_b': nrm(ks[21], (DEPTH, D_MODEL), 0.02),
    }


def reference(x, mem, w_in, b_in, attn_sinks, w_out, ln1_g, ln1_b, w_mem_q, w_mem_k, w_mem_v, w_mem_o,
              ln2_g, ln2_b, w_router, b_router, w_gate_up, b_gate_up, w_down, b_down, ln3_g, ln3_b):
    for i in range(DEPTH):
        x = _layer_norm(DEEPNORM_ALPHA * x + _hybrid_mixer(x, w_in[i], b_in[i], attn_sinks[i], w_out[i]),
                        ln1_g[i], ln1_b[i])
        x = _layer_norm(DEEPNORM_ALPHA * x + _memory_cross_attn(x, mem, w_mem_q[i], w_mem_k[i], w_mem_v[i], w_mem_o[i]),
                        ln2_g[i], ln2_b[i])
        x = _layer_norm(DEEPNORM_ALPHA * x + _moe_ffn(x, w_router[i], b_router[i], w_gate_up[i], b_gate_up[i],
                                                      w_down[i], b_down[i]),
                        ln3_g[i], ln3_b[i])
    return x
```

```python
import functools

import numpy as np
import jax
import jax.numpy as jnp
from jax import lax
from jax.experimental import pallas as pl
from jax.experimental.pallas import tpu as pltpu

F32 = jnp.float32
BF16 = jnp.bfloat16
NEG_INF = float("-inf")

LANES = 128
SUBLANES = 8
VMEM_LIMIT_BYTES = 56 * 1024 * 1024

HEAD_DIM = 64
A_Q_HEADS = 16
A_KV_HEADS = 2
A_GROUP = A_Q_HEADS // A_KV_HEADS
A_WINDOW = 128
B_HEADS = 8
MOBA_BLOCK = 256
MOBA_TOPK = 3
C_HEADS = 8
DILATED_PAIRS = ((128, 1), (512, 4), (2048, 16))
BAND = 128
A_Q_W = A_Q_HEADS * HEAD_DIM
A_KV_W = A_KV_HEADS * HEAD_DIM
B_W = B_HEADS * HEAD_DIM
C_W = C_HEADS * HEAD_DIM
IN_WIDTH = A_Q_W + 2 * A_KV_W + 3 * B_W + 3 * C_W
COL_AK = A_Q_W // LANES
COL_AV = COL_AK + A_KV_W // LANES
COL_BQ = COL_AV + A_KV_W // LANES
COL_BK = COL_BQ + B_W // LANES
COL_BV = COL_BK + B_W // LANES
COL_CQ = COL_BV + B_W // LANES
COL_CK = COL_CQ + C_W // LANES
COL_CV = COL_CK + C_W // LANES
MEM_HEADS = 4
MEM_HEAD_DIM = 128
TOPK = 4
SWIGLU_LIMIT = 7.0
SWIGLU_ALPHA = 1.702
LN_EPS = 1e-5
DEPTH = 2
DEEPNORM_ALPHA = (2 * DEPTH) ** 0.25
SLAB = 16


def _alibi_slopes(n):
    return [float(2.0 ** (-8.0 * (i + 1) / n)) for i in range(n)]


def _params(sem=None):
    return pltpu.CompilerParams(dimension_semantics=sem, vmem_limit_bytes=VMEM_LIMIT_BYTES)


def _layer_norm_rows(y, g, b):
    mu = jnp.mean(y, axis=-1, keepdims=True)
    d = y - mu
    var = jnp.mean(d * d, axis=-1, keepdims=True)
    return d * lax.rsqrt(var + LN_EPS) * g + b


def _dot_nt(a, b):
    return lax.dot_general(a, b, (((1,), (1,)), ((), ())), preferred_element_type=F32)


def _inproj_kernel(x_ref, w_ref, b_ref, o_ref):
    acc = jnp.dot(x_ref[...].astype(BF16), w_ref[...], preferred_element_type=F32)
    o_ref[...] = acc + b_ref[...]


def _inproj(x2d, w_bf, b):
    t, d = x2d.shape
    n = w_bf.shape[1]
    tm, tn = 512, n // 2
    return pl.pallas_call(
        _inproj_kernel,
        out_shape=jax.ShapeDtypeStruct((t, n), F32),
        grid=(n // tn, t // tm),
        in_specs=[
            pl.BlockSpec((tm, d), lambda j, i: (i, 0)),
            pl.BlockSpec((d, tn), lambda j, i: (0, j)),
            pl.BlockSpec((1, tn), lambda j, i: (0, j)),
        ],
        out_specs=pl.BlockSpec((tm, tn), lambda j, i: (i, j)),
        compiler_params=_params(("arbitrary", "arbitrary")),
        name="inproj",
    )(x2d, w_bf, b.reshape(1, n))


def _swa_kernel(sink_ref, q_ref, kp_ref, kc_ref, vp_ref, vc_ref, o_ref):
    i = pl.program_id(1)
    k2 = jnp.concatenate([kp_ref[0], kc_ref[0]], axis=0).astype(BF16)
    v2 = jnp.concatenate([vp_ref[0], vc_ref[0]], axis=0).astype(BF16)
    qi = lax.broadcasted_iota(jnp.int32, (BAND, 2 * BAND), 0)
    ki = lax.broadcasted_iota(jnp.int32, (BAND, 2 * BAND), 1)
    dist = BAND + qi - ki
    valid = (dist >= 0) & (dist <= A_WINDOW - 1) & ((i > 0) | (ki >= BAND))
    distf = dist.astype(F32)
    slopes = _alibi_slopes(A_Q_HEADS)
    for h in range(A_Q_HEADS):
        hk = h // A_GROUP
        kv = slice(hk * HEAD_DIM, (hk + 1) * HEAD_DIM)
        q = q_ref[0, :, h * HEAD_DIM:(h + 1) * HEAD_DIM].astype(BF16)
        s = _dot_nt(q, k2[:, kv]) * (HEAD_DIM ** -0.5)
        s = jnp.where(valid, s - slopes[h] * distf, NEG_INF)
        sink = sink_ref[h]
        m = jnp.maximum(jnp.max(s, axis=-1, keepdims=True), sink)
        p = jnp.exp(s - m)
        den = jnp.sum(p, axis=-1, keepdims=True) + jnp.exp(sink - m)
        o = jnp.dot(p.astype(BF16), v2[:, kv], preferred_element_type=F32) / den
        o_ref[0, :, h * HEAD_DIM:(h + 1) * HEAD_DIM] = o.astype(o_ref.dtype)


def _swa_attention(proj, sinks):
    b, s, _ = proj.shape
    nblk = s // BAND
    prev = lambda i: jnp.maximum(i - 1, 0)
    return pl.pallas_call(
        _swa_kernel,
        out_shape=jax.ShapeDtypeStruct((b, s, A_Q_W), BF16),
        grid=(b, nblk),
        in_specs=[
            pl.BlockSpec(memory_space=pltpu.SMEM),
            pl.BlockSpec((1, BAND, A_Q_W), lambda bi, i: (bi, i, 0)),
            pl.BlockSpec((1, BAND, LANES), lambda bi, i: (bi, prev(i), COL_AK)),
            pl.BlockSpec((1, BAND, LANES), lambda bi, i: (bi, i, COL_AK)),
            pl.BlockSpec((1, BAND, LANES), lambda bi, i: (bi, prev(i), COL_AV)),
            pl.BlockSpec((1, BAND, LANES), lambda bi, i: (bi, i, COL_AV)),
        ],
        out_specs=pl.BlockSpec((1, BAND, A_Q_W), lambda bi, i: (bi, i, 0)),
        compiler_params=_params(("arbitrary", "arbitrary")),
        name="swa_attention",
    )(sinks, proj, proj, proj, proj, proj)


def _moba_kernel(slope_ref, q_ref, k_ref, v_ref, o_ref, kmean_ref, *, nb):
    hp = pl.program_id(1)
    j = pl.program_id(2)
    blk = MOBA_BLOCK

    @pl.when(j == 0)
    def _():
        kmean_ref[...] = jnp.zeros_like(kmean_ref)
        for n in range(nb):
            kmean_ref[n:n + 1, :] = jnp.sum(k_ref[n * blk:(n + 1) * blk, :], axis=0, keepdims=True) * (1.0 / blk)

    lane = lax.broadcasted_iota(jnp.int32, (blk, LANES), 1)
    qi = lax.broadcasted_iota(jnp.int32, (blk, blk), 0)
    ki = lax.broadcasted_iota(jnp.int32, (blk, blk), 1)
    dist_own = (qi - ki).astype(F32)
    own_start = pl.multiple_of(j * blk, blk)
    for hh in range(LANES // HEAD_DIM):
        sl = slice(hh * HEAD_DIM, (hh + 1) * HEAD_DIM)
        slope = slope_ref[hp * (LANES // HEAD_DIM) + hh]
        qf = q_ref[:, sl]
        gate = lax.dot_general(qf, kmean_ref[:, sl], (((1,), (1,)), ((), ())),
                               precision=lax.Precision.HIGHEST, preferred_element_type=F32)
        g = jnp.where(lane < j, gate, NEG_INF)
        sel = jnp.zeros((blk, LANES), F32)
        for _ in range(MOBA_TOPK):
            m = jnp.max(g, axis=-1, keepdims=True)
            idx = jnp.min(jnp.where(g == m, lane, LANES), axis=-1, keepdims=True)
            hit = lane == idx
            sel = jnp.where(hit & (m > NEG_INF), 1.0, sel)
            g = jnp.where(hit, NEG_INF, g)

        qb = qf.astype(BF16)
        scale = HEAD_DIM ** -0.5
        k_own = k_ref[pl.ds(own_start, blk), sl].astype(BF16)
        v_own = v_ref[pl.ds(own_start, blk), sl].astype(BF16)
        s = _dot_nt(qb, k_own) * scale
        s = jnp.where(dist_own >= 0, s - slope * dist_own, NEG_INF)
        m0 = jnp.max(s, axis=-1, keepdims=True)
        p = jnp.exp(s - m0)
        l0 = jnp.sum(p, axis=-1, keepdims=True)
        acc0 = jnp.dot(p.astype(BF16), v_own, preferred_element_type=F32)

        def body(n, carry):
            m_run, l_run, acc = carry
            start = pl.multiple_of(n * blk, blk)
            k_n = k_ref[pl.ds(start, blk), sl].astype(BF16)
            v_n = v_ref[pl.ds(start, blk), sl].astype(BF16)
            row_sel = jnp.max(jnp.where(lane == n, sel, 0.0), axis=-1, keepdims=True) > 0.0
            dist_n = dist_own + ((j - n) * blk).astype(F32)
            sn = _dot_nt(qb, k_n) * scale
            sn = jnp.where(row_sel, sn - slope * dist_n, NEG_INF)
            m_new = jnp.maximum(m_run, jnp.max(sn, axis=-1, keepdims=True))
            corr = jnp.exp(m_run - m_new)
            pn = jnp.exp(sn - m_new)
            l_new = corr * l_run + jnp.sum(pn, axis=-1, keepdims=True)
            acc_new = corr * acc + jnp.dot(pn.astype(BF16), v_n, preferred_element_type=F32)
            return m_new, l_new, acc_new

        _, l_fin, acc_fin = lax.fori_loop(0, j, body, (m0, l0, acc0))
        o_ref[:, sl] = (acc_fin / l_fin).astype(o_ref.dtype)


def _moba_attention(proj):
    b, s, _ = proj.shape
    nb = s // MOBA_BLOCK
    npair = B_W // LANES
    slopes = jnp.asarray(_alibi_slopes(B_HEADS), F32)
    return pl.pallas_call(
        functools.partial(_moba_kernel, nb=nb),
        out_shape=jax.ShapeDtypeStruct((b, s, B_W), BF16),
        grid=(b, npair, nb),
        in_specs=[
            pl.BlockSpec(memory_space=pltpu.SMEM),
            pl.BlockSpec((None, MOBA_BLOCK, LANES), lambda bi, hp, j: (bi, j, COL_BQ + hp)),
            pl.BlockSpec((None, s, LANES), lambda bi, hp, j: (bi, 0, COL_BK + hp)),
            pl.BlockSpec((None, s, LANES), lambda bi, hp, j: (bi, 0, COL_BV + hp)),
        ],
        out_specs=pl.BlockSpec((None, MOBA_BLOCK, LANES), lambda bi, hp, j: (bi, j, hp)),
        scratch_shapes=[pltpu.VMEM((LANES, LANES), F32)],
        compiler_params=_params(("arbitrary", "arbitrary", "arbitrary")),
        name="moba_attention",
    )(slopes, proj, proj, proj)


def _dilated_kernel(slope_ref, q_ref, k_ref, v_ref, o_ref, o0, o1, o2, e0, e1, e2, *, seq):
    hp = pl.program_id(1)
    o_scr = (o0, o1, o2)
    lse_scr = (e0, e1, e2)
    nh = LANES // HEAD_DIM
    qi = lax.broadcasted_iota(jnp.int32, (BAND, 2 * BAND), 0)
    ki = lax.broadcasted_iota(jnp.int32, (BAND, 2 * BAND), 1)
    dist = BAND + qi - ki
    distf = dist.astype(F32)

    for p, (window, r) in enumerate(DILATED_PAIRS):
        nblk = seq // (r * BAND)
        in_band = (dist >= 0) & (dist <= window // r)

        def block(t, carry, p=p, r=r, in_band=in_band):
            c = t % r
            i = t // r
            cur = c + i * (BAND * r)
            prv = c + jnp.maximum(i - 1, 0) * (BAND * r)
            rows_cur = pl.ds(cur, BAND, stride=r) if r > 1 else pl.ds(cur, BAND)
            rows_prv = pl.ds(prv, BAND, stride=r) if r > 1 else pl.ds(prv, BAND)
            q = q_ref[rows_cur, :].astype(BF16)
            k2 = jnp.concatenate([k_ref[rows_prv, :], k_ref[rows_cur, :]], axis=0).astype(BF16)
            v2 = jnp.concatenate([v_ref[rows_prv, :], v_ref[rows_cur, :]], axis=0).astype(BF16)
            valid = in_band & ((i > 0) | (ki >= BAND))
            outs, lses = [], []
            for hh in range(nh):
                sl = slice(hh * HEAD_DIM, (hh + 1) * HEAD_DIM)
                slope = slope_ref[hp * nh + hh] * float(r)
                s = _dot_nt(q[:, sl], k2[:, sl]) * (HEAD_DIM ** -0.5)
                s = jnp.where(valid, s - slope * distf, NEG_INF)
                m = jnp.max(s, axis=-1, keepdims=True)
                e = jnp.exp(s - m)
                l = jnp.sum(e, axis=-1, keepdims=True)
                outs.append(jnp.dot(e.astype(BF16), v2[:, sl], preferred_element_type=F32) / l)
                lses.append(jnp.broadcast_to(m + jnp.log(l), (BAND, HEAD_DIM)))
            o_scr[p][rows_cur, :] = jnp.concatenate(outs, axis=1)
            lse_scr[p][rows_cur, :] = jnp.concatenate(lses, axis=1)
            return carry

        lax.fori_loop(0, r * nblk, block, 0)

    chunk = 512

    def combine(t, carry):
        rows = pl.ds(pl.multiple_of(t * chunk, chunk), chunk)
        ls = [e[rows, :] for e in lse_scr]
        m = jnp.maximum(jnp.maximum(ls[0], ls[1]), ls[2])
        ws = [jnp.exp(l - m) for l in ls]
        den = ws[0] + ws[1] + ws[2]
        num = ws[0] * o_scr[0][rows, :] + ws[1] * o_scr[1][rows, :] + ws[2] * o_scr[2][rows, :]
        o_ref[rows, :] = (num / den).astype(o_ref.dtype)
        return carry

    lax.fori_loop(0, seq // chunk, combine, 0)


def _dilated_attention(proj):
    b, s, _ = proj.shape
    npair = C_W // LANES
    slopes = jnp.asarray(_alibi_slopes(C_HEADS), F32)
    spec = lambda col: pl.BlockSpec((None, s, LANES), lambda bi, hp: (bi, 0, col + hp))
    return pl.pallas_call(
        functools.partial(_dilated_kernel, seq=s),
        out_shape=jax.ShapeDtypeStruct((b, s, C_W), BF16),
        grid=(b, npair),
        in_specs=[pl.BlockSpec(memory_space=pltpu.SMEM), spec(COL_CQ), spec(COL_CK), spec(COL_CV)],
        out_specs=pl.BlockSpec((None, s, LANES), lambda bi, hp: (bi, 0, hp)),
        scratch_shapes=[pltpu.VMEM((s, LANES), F32) for _ in range(6)],
        compiler_params=_params(("arbitrary", "arbitrary")),
        name="dilated_attention",
    )(slopes, proj, proj, proj)


def _outproj_ln_kernel(oa_ref, ob_ref, oc_ref, w_ref, x_ref, g_ref, b_ref, o_ref):
    acc = jnp.dot(oa_ref[...], w_ref[0:A_Q_W, :], preferred_element_type=F32)
    acc += jnp.dot(ob_ref[...], w_ref[A_Q_W:A_Q_W + B_W, :], preferred_element_type=F32)
    acc += jnp.dot(oc_ref[...], w_ref[A_Q_W + B_W:, :], preferred_element_type=F32)
    o_ref[...] = _layer_norm_rows(DEEPNORM_ALPHA * x_ref[...] + acc, g_ref[...], b_ref[...])


def _outproj_ln(oa, ob, oc, w_bf, x2d, g, b):
    t, d = x2d.shape
    tm = 512
    row = lambda w: pl.BlockSpec((tm, w), lambda i: (i, 0))
    full = lambda shp: pl.BlockSpec(shp, lambda i: (0, 0))
    return pl.pallas_call(
        _outproj_ln_kernel,
        out_shape=jax.ShapeDtypeStruct((t, d), F32),
        grid=(t // tm,),
        in_specs=[row(A_Q_W), row(B_W), row(C_W), full(w_bf.shape), row(d), full((1, d)), full((1, d))],
        out_specs=row(d),
        compiler_params=_params(("arbitrary",)),
        name="outproj_ln",
    )(oa, ob, oc, w_bf, x2d, g.reshape(1, d), b.reshape(1, d))


def _memkv_kernel(m_ref, w_ref, o_ref):
    o_ref[...] = jnp.dot(m_ref[...].astype(BF16), w_ref[...], preferred_element_type=F32).astype(o_ref.dtype)


def _mem_kv(mem2d, wkv_bf):
    t, d = mem2d.shape
    n = wkv_bf.shape[1]
    tm = min(512, t)
    return pl.pallas_call(
        _memkv_kernel,
        out_shape=jax.ShapeDtypeStruct((t, n), BF16),
        grid=(t // tm,),
        in_specs=[pl.BlockSpec((tm, d), lambda i: (i, 0)), pl.BlockSpec((d, n), lambda i: (0, 0))],
        out_specs=pl.BlockSpec((tm, n), lambda i: (i, 0)),
        compiler_params=_params(("arbitrary",)),
        name="mem_kv",
    )(mem2d, wkv_bf)


def _split_bf16(a):
    hi = a.astype(BF16)
    lo = (a - hi.astype(F32)).astype(BF16)
    return hi, lo


def _cross_ln_router_kernel(x_ref, kv_ref, wq_ref, wo_ref, g_ref, b_ref, wr_ref, br_ref,
                            x2_ref, slab_ref, idx_ref, gate_ref, *, n_experts):
    x = x_ref[...]
    tq = x.shape[0]
    mw = MEM_HEADS * MEM_HEAD_DIM
    q = jnp.dot(x.astype(BF16), wq_ref[...], preferred_element_type=F32).astype(BF16)
    heads = []
    for h in range(MEM_HEADS):
        sl = slice(h * MEM_HEAD_DIM, (h + 1) * MEM_HEAD_DIM)
        s = _dot_nt(q[:, sl], kv_ref[:, sl]) * (MEM_HEAD_DIM ** -0.5)
        m = jnp.max(s, axis=-1, keepdims=True)
        e = jnp.exp(s - m)
        l = jnp.sum(e, axis=-1, keepdims=True)
        o = jnp.dot(e.astype(BF16), kv_ref[:, mw + h * MEM_HEAD_DIM:mw + (h + 1) * MEM_HEAD_DIM],
                    preferred_element_type=F32) / l
        heads.append(o.astype(BF16))
    o = jnp.concatenate(heads, axis=1)
    y = jnp.dot(o, wo_ref[...], preferred_element_type=F32)
    x2 = _layer_norm_rows(DEEPNORM_ALPHA * x + y, g_ref[...], b_ref[...])
    x2_ref[...] = x2
    for c in range(SLAB):
        slab_ref[pl.ds(c, tq, stride=SLAB), :] = x2[:, c * LANES:(c + 1) * LANES]

    xh, xl = _split_bf16(x2)
    wh, wl = _split_bf16(wr_ref[...])
    logits = (jnp.dot(xh, wh, preferred_element_type=F32) + jnp.dot(xh, wl, preferred_element_type=F32)
              + jnp.dot(xl, wh, preferred_element_type=F32)) + br_ref[...]
    lane = lax.broadcasted_iota(jnp.int32, (tq, LANES), 1)
    g = jnp.where(lane < n_experts, logits, NEG_INF)
    idx_out = jnp.zeros((tq, LANES), jnp.int32)
    val_out = jnp.zeros((tq, LANES), F32)
    top = None
    den = jnp.zeros((tq, 1), F32)
    for r in range(TOPK):
        m = jnp.max(g, axis=-1, keepdims=True)
        idx = jnp.min(jnp.where(g == m, lane, LANES), axis=-1, keepdims=True)
        g = jnp.where(lane == idx, NEG_INF, g)
        top = m if top is None else top
        e = jnp.exp(m - top)
        den = den + e
        idx_out = jnp.where(lane == r, idx, idx_out)
        val_out = jnp.where(lane == r, e, val_out)
    idx_ref[...] = idx_out
    gate_ref[...] = val_out / den


def _cross_ln_router(x3d, kv, wq_bf, wo_bf, g, b, w_router, b_router):
    bsz, s, d = x3d.shape
    m = kv.shape[1]
    n_experts = w_router.shape[1]
    tq = 512
    nq = s // tq
    wr = jnp.pad(w_router, ((0, 0), (0, LANES - n_experts)))
    br = jnp.pad(b_router, (0, LANES - n_experts)).reshape(1, LANES)
    full = lambda shp: pl.BlockSpec(shp, lambda bi, i: (0,) * len(shp))
    t = bsz * s
    return pl.pallas_call(
        functools.partial(_cross_ln_router_kernel, n_experts=n_experts),
        out_shape=(jax.ShapeDtypeStruct((t, d), F32), jax.ShapeDtypeStruct((t * SLAB, LANES), F32),
                   jax.ShapeDtypeStruct((t, LANES), jnp.int32), jax.ShapeDtypeStruct((t, LANES), F32)),
        grid=(bsz, nq),
        in_specs=[
            pl.BlockSpec((None, tq, d), lambda bi, i: (bi, i, 0)),
            pl.BlockSpec((None, m, kv.shape[2]), lambda bi, i: (bi, 0, 0)),
            full(wq_bf.shape), full(wo_bf.shape), full((1, d)), full((1, d)), full(wr.shape), full((1, LANES)),
        ],
        out_specs=(
            pl.BlockSpec((tq, d), lambda bi, i: (bi * nq + i, 0)),
            pl.BlockSpec((tq * SLAB, LANES), lambda bi, i: (bi * nq + i, 0)),
            pl.BlockSpec((tq, LANES), lambda bi, i: (bi * nq + i, 0)),
            pl.BlockSpec((tq, LANES), lambda bi, i: (bi * nq + i, 0)),
        ),
        compiler_params=_params(("arbitrary", "arbitrary")),
        name="cross_ln_router",
    )(x3d, kv, wq_bf, wo_bf, g.reshape(1, d), b.reshape(1, d), wr, br)


def _slab_copy(src_hbm, src_row, dst, dst_row, sem):
    return pltpu.make_async_copy(src_hbm.at[pl.ds(src_row * SLAB, SLAB), :],
                                 dst.at[pl.ds(dst_row * SLAB, SLAB), :], sem)


def _gather_kernel(tok_ref, x_hbm, o_ref, slab, sem, *, tm):
    def issue(r, carry):
        _slab_copy(x_hbm, tok_ref[0, r], slab, r, sem).start()
        return carry

    lax.fori_loop(0, tm, issue, 0)

    def drain(r, carry):
        _slab_copy(x_hbm, 0, slab, r, sem).wait()
        return carry

    lax.fori_loop(0, tm, drain, 0)
    for c in range(SLAB):
        o_ref[:, c * LANES:(c + 1) * LANES] = slab[pl.ds(c, tm, stride=SLAB), :].astype(o_ref.dtype)


def _dispatch_gather(row_tok, x_slab, n_rows, tm, d):
    return pl.pallas_call(
        functools.partial(_gather_kernel, tm=tm),
        out_shape=jax.ShapeDtypeStruct((n_rows, d), BF16),
        grid=(n_rows // tm,),
        in_specs=[pl.BlockSpec((None, 1, tm), lambda i: (i, 0, 0), memory_space=pltpu.SMEM),
                  pl.BlockSpec(memory_space=pl.ANY)],
        out_specs=pl.BlockSpec((tm, d), lambda i: (i, 0)),
        scratch_shapes=[pltpu.VMEM((tm * SLAB, LANES), F32), pltpu.SemaphoreType.DMA(())],
        compiler_params=_params(("arbitrary",)),
        name="moe_dispatch",
    )(row_tok.reshape(n_rows // tm, 1, tm), x_slab)


def _expert_kernel(te_ref, tv_ref, x_ref, wg_ref, wu_ref, bg_ref, bu_ref, wd_ref, bd_ref, o_ref, acc_ref, *, tm):
    i = pl.program_id(0)
    f = pl.program_id(1)
    nf = pl.num_programs(1)

    @pl.when((tv_ref[i] == 0) & (f == nf - 1))
    def _():
        o_ref[...] = jnp.zeros_like(o_ref)

    @pl.when(tv_ref[i] > 0)
    def _():
        x = x_ref[...]
        gate = jnp.dot(x, wg_ref[...], preferred_element_type=F32) + bg_ref[...]
        up = jnp.dot(x, wu_ref[...], preferred_element_type=F32) + bu_ref[...]
        gate = jnp.minimum(gate, SWIGLU_LIMIT)
        up = jnp.clip(up, -SWIGLU_LIMIT, SWIGLU_LIMIT)
        act = (up + 1.0) * (gate * jax.nn.sigmoid(SWIGLU_ALPHA * gate))
        part = jnp.dot(act.astype(BF16), wd_ref[...], preferred_element_type=F32)

        @pl.when(f == 0)
        def _():
            acc_ref[...] = part + bd_ref[...]

        @pl.when(f > 0)
        def _():
            acc_ref[...] += part

        @pl.when(f == nf - 1)
        def _():
            for c in range(SLAB):
                o_ref[pl.ds(c, tm, stride=SLAB), :] = acc_ref[:, c * LANES:(c + 1) * LANES]


def _expert_mlp(tile_expert, tile_valid, xs, wgu_bf, b_gu, wd_bf, b_down, tm, tf):
    n_rows, d = xs.shape
    n_exp, _, two_ff = wgu_bf.shape
    d_ff = two_ff // 2
    nf = d_ff // tf
    wf = lambda i, f, tv: jnp.where(tv[i] > 0, f, nf - 1)
    return pl.pallas_call(
        functools.partial(_expert_kernel, tm=tm),
        out_shape=jax.ShapeDtypeStruct((n_rows * SLAB, LANES), F32),
        grid_spec=pltpu.PrefetchScalarGridSpec(
            num_scalar_prefetch=2,
            grid=(n_rows // tm, nf),
            in_specs=[
                pl.BlockSpec((tm, d), lambda i, f, te, tv: (i, 0)),
                pl.BlockSpec((None, d, tf), lambda i, f, te, tv: (te[i], 0, wf(i, f, tv))),
                pl.BlockSpec((None, d, tf), lambda i, f, te, tv: (te[i], 0, nf + wf(i, f, tv))),
                pl.BlockSpec((None, 1, tf), lambda i, f, te, tv: (te[i], 0, wf(i, f, tv))),
                pl.BlockSpec((None, 1, tf), lambda i, f, te, tv: (te[i], 0, nf + wf(i, f, tv))),
                pl.BlockSpec((None, tf, d), lambda i, f, te, tv: (te[i], wf(i, f, tv), 0)),
                pl.BlockSpec((None, 1, d), lambda i, f, te, tv: (te[i], 0, 0)),
            ],
            out_specs=pl.BlockSpec((tm * SLAB, LANES), lambda i, f, te, tv: (i, 0)),
            scratch_shapes=[pltpu.VMEM((tm, d), F32)],
        ),
        compiler_params=_params(("arbitrary", "arbitrary")),
        name="moe_experts",
    )(tile_expert, tile_valid, xs, wgu_bf, wgu_bf, b_gu.reshape(n_exp, 1, two_ff), b_gu.reshape(n_exp, 1, two_ff),
      wd_bf, b_down.reshape(n_exp, 1, d))


def _combine_ln_kernel(pos_ref, y_hbm, gate_ref, x_ref, g_ref, b_ref, o_ref, slab, y_scr, sem, *, tt):
    n = tt * TOPK

    def issue(r, carry):
        k = r // tt
        t = r - k * tt
        _slab_copy(y_hbm, pos_ref[0, t * TOPK + k], slab, r, sem).start()
        return carry

    lax.fori_loop(0, n, issue, 0)

    def drain(r, carry):
        _slab_copy(y_hbm, 0, slab, r, sem).wait()
        return carry

    lax.fori_loop(0, n, drain, 0)
    gates = gate_ref[...]
    for c in range(SLAB):
        acc = jnp.zeros((tt, LANES), F32)
        for k in range(TOPK):
            acc += gates[:, k:k + 1] * slab[pl.ds(k * tt * SLAB + c, tt, stride=SLAB), :]
        y_scr[:, c * LANES:(c + 1) * LANES] = acc
    o_ref[...] = _layer_norm_rows(DEEPNORM_ALPHA * x_ref[...] + y_scr[...], g_ref[...], b_ref[...])


def _combine_ln(pos, y_slab, gates, x2d, g, b):
    t, d = x2d.shape
    tt = 256
    return pl.pallas_call(
        functools.partial(_combine_ln_kernel, tt=tt),
        out_shape=jax.ShapeDtypeStruct((t, d), F32),
        grid=(t // tt,),
        in_specs=[
            pl.BlockSpec((None, 1, tt * TOPK), lambda i: (i, 0, 0), memory_space=pltpu.SMEM),
            pl.BlockSpec(memory_space=pl.ANY),
            pl.BlockSpec((tt, LANES), lambda i: (i, 0)),
            pl.BlockSpec((tt, d), lambda i: (i, 0)),
            pl.BlockSpec((1, d), lambda i: (0, 0)),
            pl.BlockSpec((1, d), lambda i: (0, 0)),
        ],
        out_specs=pl.BlockSpec((tt, d), lambda i: (i, 0)),
        scratch_shapes=[pltpu.VMEM((tt * TOPK * SLAB, LANES), F32), pltpu.VMEM((tt, d), F32),
                        pltpu.SemaphoreType.DMA(())],
        compiler_params=_params(("arbitrary",)),
        name="moe_combine_ln",
    )(pos.reshape(t // tt, 1, tt * TOPK), y_slab, gates, x2d, g.reshape(1, d), b.reshape(1, d))


def _routing_tables(top_idx, n_experts, tm):
    t = top_idx.shape[0]
    n_assign = t * TOPK
    e_flat = top_idx.reshape(-1)
    onehot = (e_flat[:, None] == jnp.arange(n_experts, dtype=jnp.int32)[None, :]).astype(jnp.int32)
    csum = jnp.cumsum(onehot, axis=0)
    counts = csum[-1]
    rank = jnp.sum((csum - onehot) * onehot, axis=1)
    padded = (counts + tm - 1) // tm * tm
    p_end = jnp.cumsum(padded)
    p_start = p_end - padded
    start = jnp.cumsum(counts) - counts
    pos = (p_start[e_flat] + rank).astype(jnp.int32)
    n_tiles = (n_assign + n_experts * (tm - 1)) // tm
    n_rows = n_tiles * tm
    order = jnp.argsort(e_flat)
    tile_expert = jnp.minimum(jnp.searchsorted(p_end, jnp.arange(n_tiles) * tm, side="right"),
                              n_experts - 1).astype(jnp.int32)
    tile_valid = (jnp.arange(n_tiles) * tm < p_end[-1]).astype(jnp.int32)
    rows = jnp.arange(n_rows)
    row_e = jnp.repeat(tile_expert, tm)
    off = rows - p_start[row_e]
    real = (off < counts[row_e]) & (rows < p_end[-1])
    src = jnp.clip(start[row_e] + off, 0, n_assign - 1)
    row_tok = jnp.where(real, order[src] // TOPK, 0).astype(jnp.int32)
    return row_tok, pos, tile_expert, tile_valid, n_rows


MOE_TM = 512
MOE_TF = 512


def kernel(x, mem, w_in, b_in, attn_sinks, w_out, ln1_g, ln1_b, w_mem_q, w_mem_k, w_mem_v, w_mem_o, ln2_g, ln2_b,
           w_router, b_router, w_gate_up, b_gate_up, w_down, b_down, ln3_g, ln3_b):
    bsz, s, d = x.shape
    t = bsz * s
    depth = w_in.shape[0]
    n_experts = w_router.shape[-1]
    mem2d = mem.reshape(bsz * mem.shape[1], d)
    x2d = x.reshape(t, d)
    for i in range(depth):
        proj = _inproj(x2d, w_in[i].astype(BF16), b_in[i]).reshape(bsz, s, IN_WIDTH)
        oa = _swa_attention(proj, attn_sinks[i])
        ob = _moba_attention(proj)
        oc = _dilated_attention(proj)
        x1 = _outproj_ln(oa.reshape(t, A_Q_W), ob.reshape(t, B_W), oc.reshape(t, C_W), w_out[i].astype(BF16), x2d,
                         ln1_g[i], ln1_b[i])
        wkv = jnp.concatenate([w_mem_k[i], w_mem_v[i]], axis=1).astype(BF16)
        kv = _mem_kv(mem2d, wkv).reshape(bsz, mem.shape[1], wkv.shape[1])
        x2, x2_slab, top_idx, gates = _cross_ln_router(x1.reshape(bsz, s, d), kv, w_mem_q[i].astype(BF16),
                                                       w_mem_o[i].astype(BF16), ln2_g[i], ln2_b[i],
                                                       w_router[i], b_router[i])
        row_tok, pos, tile_expert, tile_valid, n_rows = _routing_tables(top_idx[:, :TOPK], n_experts, MOE_TM)
        xs = _dispatch_gather(row_tok, x2_slab, n_rows, MOE_TM, d)
        y_slab = _expert_mlp(tile_expert, tile_valid, xs, w_gate_up[i].astype(BF16), b_gate_up[i],
                             w_down[i].astype(BF16), b_down[i], MOE_TM, min(MOE_TF, w_down.shape[2]))
        x2d = _combine_ln(pos, y_slab, gates, x2, ln3_g[i], ln3_b[i])
    return x2d.reshape(bsz, s, d)
```

```python
import functools

import numpy as np
import jax
import jax.numpy as jnp
from jax import lax
from jax.experimental import pallas as pl
from jax.experimental.pallas import tpu as pltpu

F32 = jnp.float32
BF16 = jnp.bfloat16
NEG_INF = float("-inf")

LANES = 128
SUBLANES = 8
VMEM_LIMIT_BYTES = 56 * 1024 * 1024

HEAD_DIM = 64
A_Q_HEADS = 16
A_KV_HEADS = 2
A_GROUP = A_Q_HEADS // A_KV_HEADS
A_WINDOW = 128
B_HEADS = 8
MOBA_BLOCK = 256
MOBA_TOPK = 3
C_HEADS = 8
DILATED_PAIRS = ((128, 1), (512, 4), (2048, 16))
BAND = 128
A_Q_W = A_Q_HEADS * HEAD_DIM
A_KV_W = A_KV_HEADS * HEAD_DIM
B_W = B_HEADS * HEAD_DIM
C_W = C_HEADS * HEAD_DIM
IN_WIDTH = A_Q_W + 2 * A_KV_W + 3 * B_W + 3 * C_W
COL_AK = A_Q_W // LANES
COL_AV = COL_AK + A_KV_W // LANES
COL_BQ = COL_AV + A_KV_W // LANES
COL_BK = COL_BQ + B_W // LANES
COL_BV = COL_BK + B_W // LANES
COL_CQ = COL_BV + B_W // LANES
COL_CK = COL_CQ + C_W // LANES
COL_CV = COL_CK + C_W // LANES
MEM_HEADS = 4
MEM_HEAD_DIM = 128
TOPK = 4
SWIGLU_LIMIT = 7.0
SWIGLU_ALPHA = 1.702
LN_EPS = 1e-5
DEPTH = 2
DEEPNORM_ALPHA = (2 * DEPTH) ** 0.25
SLAB = 16
PITCH = 24


def _alibi_slopes(n):
    return [float(2.0 ** (-8.0 * (i + 1) / n)) for i in range(n)]


def _params(sem=None):
    return pltpu.CompilerParams(dimension_semantics=sem, vmem_limit_bytes=VMEM_LIMIT_BYTES)


def _layer_norm_rows(y, g, b):
    mu = jnp.mean(y, axis=-1, keepdims=True)
    d = y - mu
    var = jnp.mean(d * d, axis=-1, keepdims=True)
    return d * lax.rsqrt(var + LN_EPS) * g + b


def _dot_nt(a, b):
    return lax.dot_general(a, b, (((1,), (1,)), ((), ())), preferred_element_type=F32)


def _inproj_kernel(x_ref, w_ref, b_ref, o_ref):
    acc = jnp.dot(x_ref[...].astype(BF16), w_ref[...], preferred_element_type=F32)
    o_ref[...] = acc + b_ref[...]


def _inproj(x2d, w_bf, b):
    t, d = x2d.shape
    n = w_bf.shape[1]
    tm, tn = 512, n // 2
    return pl.pallas_call(
        _inproj_kernel,
        out_shape=jax.ShapeDtypeStruct((t, n), F32),
        grid=(n // tn, t // tm),
        in_specs=[
            pl.BlockSpec((tm, d), lambda j, i: (i, 0)),
            pl.BlockSpec((d, tn), lambda j, i: (0, j)),
            pl.BlockSpec((1, tn), lambda j, i: (0, j)),
        ],
        out_specs=pl.BlockSpec((tm, tn), lambda j, i: (i, j)),
        compiler_params=_params(("arbitrary", "arbitrary")),
        name="inproj",
    )(x2d, w_bf, b.reshape(1, n))


def _swa_kernel(sink_ref, q_ref, kp_ref, kc_ref, vp_ref, vc_ref, o_ref):
    i = pl.program_id(1)
    k2 = jnp.concatenate([kp_ref[0], kc_ref[0]], axis=0).astype(BF16)
    v2 = jnp.concatenate([vp_ref[0], vc_ref[0]], axis=0).astype(BF16)
    qi = lax.broadcasted_iota(jnp.int32, (BAND, 2 * BAND), 0)
    ki = lax.broadcasted_iota(jnp.int32, (BAND, 2 * BAND), 1)
    dist = BAND + qi - ki
    valid = (dist >= 0) & (dist <= A_WINDOW - 1) & ((i > 0) | (ki >= BAND))
    distf = dist.astype(F32)
    slopes = _alibi_slopes(A_Q_HEADS)
    for h in range(A_Q_HEADS):
        hk = h // A_GROUP
        kv = slice(hk * HEAD_DIM, (hk + 1) * HEAD_DIM)
        q = q_ref[0, :, h * HEAD_DIM:(h + 1) * HEAD_DIM].astype(BF16)
        s = _dot_nt(q, k2[:, kv]) * (HEAD_DIM ** -0.5)
        s = jnp.where(valid, s - slopes[h] * distf, NEG_INF)
        sink = sink_ref[h]
        m = jnp.maximum(jnp.max(s, axis=-1, keepdims=True), sink)
        p = jnp.exp(s - m)
        den = jnp.sum(p, axis=-1, keepdims=True) + jnp.exp(sink - m)
        o = jnp.dot(p.astype(BF16), v2[:, kv], preferred_element_type=F32) / den
        o_ref[0, :, h * HEAD_DIM:(h + 1) * HEAD_DIM] = o.astype(o_ref.dtype)


def _swa_attention(proj, sinks):
    b, s, _ = proj.shape
    nblk = s // BAND
    prev = lambda i: jnp.maximum(i - 1, 0)
    return pl.pallas_call(
        _swa_kernel,
        out_shape=jax.ShapeDtypeStruct((b, s, A_Q_W), BF16),
        grid=(b, nblk),
        in_specs=[
            pl.BlockSpec(memory_space=pltpu.SMEM),
            pl.BlockSpec((1, BAND, A_Q_W), lambda bi, i: (bi, i, 0)),
            pl.BlockSpec((1, BAND, LANES), lambda bi, i: (bi, prev(i), COL_AK)),
            pl.BlockSpec((1, BAND, LANES), lambda bi, i: (bi, i, COL_AK)),
            pl.BlockSpec((1, BAND, LANES), lambda bi, i: (bi, prev(i), COL_AV)),
            pl.BlockSpec((1, BAND, LANES), lambda bi, i: (bi, i, COL_AV)),
        ],
        out_specs=pl.BlockSpec((1, BAND, A_Q_W), lambda bi, i: (bi, i, 0)),
        compiler_params=_params(("arbitrary", "arbitrary")),
        name="swa_attention",
    )(sinks, proj, proj, proj, proj, proj)


MASK_VALUE = -1e30
PEN_COLS = 16


def _moba_kernel(slope_ref, q_ref, k_ref, v_ref, o_ref, kmean_ref, kaug_ref, vaug_ref, *, nb):
    hp = pl.program_id(1)
    j = pl.program_id(2)
    blk = MOBA_BLOCK
    nh = LANES // HEAD_DIM

    @pl.when(j == 0)
    def _():
        kmean_ref[...] = jnp.zeros_like(kmean_ref)
        lane_k = lax.broadcasted_iota(jnp.int32, (blk, LANES), 1)
        row_v = lax.broadcasted_iota(jnp.int32, (HEAD_DIM, blk), 0)
        ones_row = jnp.where(row_v == 0, 1.0, 0.0)
        for n in range(nb):
            kblk = k_ref[n * blk:(n + 1) * blk, :]
            kmean_ref[n:n + 1, :] = jnp.sum(kblk, axis=0, keepdims=True) * (1.0 / blk)
            onehot = jnp.where((lane_k == HEAD_DIM + n) | (lane_k == HEAD_DIM + PEN_COLS + n), 1.0, 0.0)
            vt = v_ref[n * blk:(n + 1) * blk, :].T
            for hh in range(nh):
                sl = slice(hh * HEAD_DIM, (hh + 1) * HEAD_DIM)
                k_low = kblk if hh == 0 else pltpu.roll(kblk, HEAD_DIM, 1)
                kaug_ref[hh, n] = jnp.where(lane_k < HEAD_DIM, k_low, onehot).astype(BF16)
                vaug_ref[hh, n] = jnp.concatenate([vt[sl, :], ones_row], axis=0).astype(BF16)

    ki = lax.broadcasted_iota(jnp.int32, (blk, blk), 0)
    qi = lax.broadcasted_iota(jnp.int32, (blk, blk), 1)
    rel = (qi - ki).astype(F32)
    causal = qi >= ki
    blk_id = lax.broadcasted_iota(jnp.int32, (PEN_COLS, blk), 0)
    qt_both = q_ref[...].T
    qt_aug, bias, m0, o0 = [], [], [], []
    for hh in range(nh):
        sl = slice(hh * HEAD_DIM, (hh + 1) * HEAD_DIM)
        slope = slope_ref[hp * nh + hh]
        gate = lax.dot_general(kmean_ref[:, sl], q_ref[:, sl], (((1,), (1,)), ((), ())),
                               precision=lax.Precision.HIGHEST, preferred_element_type=F32)
        g = jnp.where(blk_id < j, gate[:PEN_COLS, :], NEG_INF)
        sel = jnp.zeros((PEN_COLS, blk), jnp.bool_)
        for _ in range(MOBA_TOPK):
            m = jnp.max(g, axis=0, keepdims=True)
            idx = jnp.min(jnp.where(g == m, blk_id, PEN_COLS), axis=0, keepdims=True)
            hit = blk_id == idx
            sel = sel | (hit & (m > NEG_INF))
            g = jnp.where(hit, NEG_INF, g)
        block_off = -slope * ((j - blk_id) * blk).astype(F32)
        pen = jnp.where(blk_id == j, 0.0, jnp.where(sel, block_off, MASK_VALUE))
        pen_hi = pen.astype(BF16)
        pen_lo = (pen - pen_hi.astype(F32)).astype(BF16)
        qa = jnp.concatenate([(qt_both[sl, :] * (HEAD_DIM ** -0.5)).astype(BF16), pen_hi, pen_lo,
                              jnp.zeros((HEAD_DIM - 2 * PEN_COLS, blk), BF16)], axis=0)
        b_t = -slope * rel
        s = jnp.where(causal, jnp.dot(kaug_ref[hh, j], qa, preferred_element_type=F32) + b_t, MASK_VALUE)
        m_own = jnp.max(s, axis=0, keepdims=True)
        p = jnp.exp(s - m_own).astype(BF16)
        qt_aug.append(qa)
        bias.append(b_t)
        m0.append(m_own)
        o0.append(jnp.dot(vaug_ref[hh, j], p, preferred_element_type=F32))

    def scores(n):
        return tuple(jnp.dot(kaug_ref[hh, n], qt_aug[hh], preferred_element_type=F32) + bias[hh] for hh in range(nh))

    def body(n, carry):
        ms, os_, s_cur = carry
        s_next = scores(jnp.minimum(n + 1, nb - 1))
        new_m, ps = [], []
        for hh in range(nh):
            m_new = jnp.maximum(ms[hh], jnp.max(s_cur[hh], axis=0, keepdims=True))
            ps.append(jnp.exp(s_cur[hh] - m_new).astype(BF16))
            new_m.append(m_new)
        new_o = [jnp.exp(ms[hh] - new_m[hh]) * os_[hh] + jnp.dot(vaug_ref[hh, n], ps[hh], preferred_element_type=F32)
                 for hh in range(nh)]
        return tuple(new_m), tuple(new_o), s_next

    _, o_fin, _ = lax.fori_loop(0, j, body, (tuple(m0), tuple(o0), scores(0)))
    outs = [o[:HEAD_DIM, :] / o[HEAD_DIM:HEAD_DIM + 1, :] for o in o_fin]
    o_ref[...] = jnp.concatenate(outs, axis=0).T.astype(o_ref.dtype)


def _moba_attention(proj):
    b, s, _ = proj.shape
    nb = s // MOBA_BLOCK
    assert nb <= PEN_COLS
    npair = B_W // LANES
    nh = LANES // HEAD_DIM
    slopes = jnp.asarray(_alibi_slopes(B_HEADS), F32)
    return pl.pallas_call(
        functools.partial(_moba_kernel, nb=nb),
        out_shape=jax.ShapeDtypeStruct((b, s, B_W), BF16),
        grid=(b, npair, nb),
        in_specs=[
            pl.BlockSpec(memory_space=pltpu.SMEM),
            pl.BlockSpec((None, MOBA_BLOCK, LANES), lambda bi, hp, j: (bi, j, COL_BQ + hp)),
            pl.BlockSpec((None, s, LANES), lambda bi, hp, j: (bi, 0, COL_BK + hp)),
            pl.BlockSpec((None, s, LANES), lambda bi, hp, j: (bi, 0, COL_BV + hp)),
        ],
        out_specs=pl.BlockSpec((None, MOBA_BLOCK, LANES), lambda bi, hp, j: (bi, j, hp)),
        scratch_shapes=[pltpu.VMEM((LANES, LANES), F32),
                        pltpu.VMEM((nh, nb, MOBA_BLOCK, LANES), BF16),
                        pltpu.VMEM((nh, nb, LANES, MOBA_BLOCK), BF16)],
        compiler_params=_params(("arbitrary", "arbitrary", "arbitrary")),
        name="moba_attention",
    )(slopes, proj, proj, proj)


DIL_GROUP = 4


def _dilated_kernel(slope_ref, q_ref, k_ref, v_ref, o_ref, o0, o1, o2, e0, e1, e2, *, seq):
    hp = pl.program_id(1)
    o_scr = (o0, o1, o2)
    lse_scr = (e0, e1, e2)
    nh = LANES // HEAD_DIM
    ki = lax.broadcasted_iota(jnp.int32, (2 * BAND, BAND), 0)
    qi = lax.broadcasted_iota(jnp.int32, (2 * BAND, BAND), 1)
    dist = BAND + qi - ki
    distf = dist.astype(F32)
    first_fix = jnp.where(ki >= BAND, 0.0, MASK_VALUE)
    row_v = lax.broadcasted_iota(jnp.int32, (HEAD_DIM, 2 * BAND), 0)
    ones_row = jnp.where(row_v == 0, 1.0, 0.0)

    for p, (window, r) in enumerate(DILATED_PAIRS):
        nblk = seq // (r * BAND)
        in_band = (dist >= 0) & (dist <= window // r)
        biases = [jnp.where(in_band, -(slope_ref[hp * nh + hh] * float(r)) * distf, MASK_VALUE) for hh in range(nh)]

        def group(t, carry, p=p, r=r, biases=biases):
            rows, vts, ss = [], [], []
            for gidx in range(DIL_GROUP):
                tb = t * DIL_GROUP + gidx
                c = tb % r
                i = tb // r
                cur = c + i * (BAND * r)
                prv = c + jnp.maximum(i - 1, 0) * (BAND * r)
                rows_cur = pl.ds(cur, BAND, stride=r) if r > 1 else pl.ds(cur, BAND)
                rows_prv = pl.ds(prv, BAND, stride=r) if r > 1 else pl.ds(prv, BAND)
                q = (q_ref[rows_cur, :] * (HEAD_DIM ** -0.5)).astype(BF16)
                k2 = jnp.concatenate([k_ref[rows_prv, :], k_ref[rows_cur, :]], axis=0).astype(BF16)
                fix = jnp.where(i > 0, 0.0, first_fix)
                for hh in range(nh):
                    sl = slice(hh * HEAD_DIM, (hh + 1) * HEAD_DIM)
                    ss.append(_dot_nt(k2[:, sl], q[:, sl]) + biases[hh] + fix)
                rows.append(rows_cur)
                vts.append(jnp.concatenate([v_ref[rows_prv, :], v_ref[rows_cur, :]], axis=0).T)
            ms = [jnp.max(sc, axis=0, keepdims=True) for sc in ss]
            es = [jnp.exp(sc - m).astype(BF16) for sc, m in zip(ss, ms)]
            os_ = []
            for gidx in range(DIL_GROUP):
                for hh in range(nh):
                    sl = slice(hh * HEAD_DIM, (hh + 1) * HEAD_DIM)
                    vaug = jnp.concatenate([vts[gidx][sl, :], ones_row], axis=0).astype(BF16)
                    os_.append(jnp.dot(vaug, es[gidx * nh + hh], preferred_element_type=F32))
            for gidx in range(DIL_GROUP):
                outs, lses = [], []
                for hh in range(nh):
                    o = os_[gidx * nh + hh]
                    l = o[HEAD_DIM:HEAD_DIM + 1, :]
                    outs.append(o[:HEAD_DIM, :] / l)
                    lses.append(jnp.broadcast_to(ms[gidx * nh + hh] + jnp.log(l), (HEAD_DIM, BAND)))
                o_scr[p][rows[gidx], :] = jnp.concatenate(outs, axis=0).T
                lse_scr[p][rows[gidx], :] = jnp.concatenate(lses, axis=0).T
            return carry

        lax.fori_loop(0, r * nblk // DIL_GROUP, group, 0)

    chunk = 512

    def combine(t, carry):
        rows = pl.ds(pl.multiple_of(t * chunk, chunk), chunk)
        ls = [e[rows, :] for e in lse_scr]
        m = jnp.maximum(jnp.maximum(ls[0], ls[1]), ls[2])
        ws = [jnp.exp(l - m) for l in ls]
        den = ws[0] + ws[1] + ws[2]
        num = ws[0] * o_scr[0][rows, :] + ws[1] * o_scr[1][rows, :] + ws[2] * o_scr[2][rows, :]
        o_ref[rows, :] = (num / den).astype(o_ref.dtype)
        return carry

    lax.fori_loop(0, seq // chunk, combine, 0)


def _dilated_attention(proj):
    b, s, _ = proj.shape
    npair = C_W // LANES
    slopes = jnp.asarray(_alibi_slopes(C_HEADS), F32)
    spec = lambda col: pl.BlockSpec((None, s, LANES), lambda bi, hp: (bi, 0, col + hp))
    return pl.pallas_call(
        functools.partial(_dilated_kernel, seq=s),
        out_shape=jax.ShapeDtypeStruct((b, s, C_W), BF16),
        grid=(b, npair),
        in_specs=[pl.BlockSpec(memory_space=pltpu.SMEM), spec(COL_CQ), spec(COL_CK), spec(COL_CV)],
        out_specs=pl.BlockSpec((None, s, LANES), lambda bi, hp: (bi, 0, hp)),
        scratch_shapes=[pltpu.VMEM((s, LANES), F32) for _ in range(6)],
        compiler_params=_params(("arbitrary", "arbitrary")),
        name="dilated_attention",
    )(slopes, proj, proj, proj)


def _outproj_ln_kernel(oa_ref, ob_ref, oc_ref, w_ref, x_ref, g_ref, b_ref, o_ref):
    acc = jnp.dot(oa_ref[...], w_ref[0:A_Q_W, :], preferred_element_type=F32)
    acc += jnp.dot(ob_ref[...], w_ref[A_Q_W:A_Q_W + B_W, :], preferred_element_type=F32)
    acc += jnp.dot(oc_ref[...], w_ref[A_Q_W + B_W:, :], preferred_element_type=F32)
    o_ref[...] = _layer_norm_rows(DEEPNORM_ALPHA * x_ref[...] + acc, g_ref[...], b_ref[...])


def _outproj_ln(oa, ob, oc, w_bf, x2d, g, b):
    t, d = x2d.shape
    tm = 512
    row = lambda w: pl.BlockSpec((tm, w), lambda i: (i, 0))
    full = lambda shp: pl.BlockSpec(shp, lambda i: (0, 0))
    return pl.pallas_call(
        _outproj_ln_kernel,
        out_shape=jax.ShapeDtypeStruct((t, d), F32),
        grid=(t // tm,),
        in_specs=[row(A_Q_W), row(B_W), row(C_W), full(w_bf.shape), row(d), full((1, d)), full((1, d))],
        out_specs=row(d),
        compiler_params=_params(("arbitrary",)),
        name="outproj_ln",
    )(oa, ob, oc, w_bf, x2d, g.reshape(1, d), b.reshape(1, d))


def _memkv_kernel(m_ref, w_ref, o_ref):
    o_ref[...] = jnp.dot(m_ref[...].astype(BF16), w_ref[...], preferred_element_type=F32).astype(o_ref.dtype)


def _mem_kv(mem2d, wkv_bf):
    t, d = mem2d.shape
    n = wkv_bf.shape[1]
    tm = min(512, t)
    return pl.pallas_call(
        _memkv_kernel,
        out_shape=jax.ShapeDtypeStruct((t, n), BF16),
        grid=(t // tm,),
        in_specs=[pl.BlockSpec((tm, d), lambda i: (i, 0)), pl.BlockSpec((d, n), lambda i: (0, 0))],
        out_specs=pl.BlockSpec((tm, n), lambda i: (i, 0)),
        compiler_params=_params(("arbitrary",)),
        name="mem_kv",
    )(mem2d, wkv_bf)


def _split_bf16(a):
    hi = a.astype(BF16)
    lo = (a - hi.astype(F32)).astype(BF16)
    return hi, lo


def _cross_ln_router_kernel(x_ref, kv_ref, wq_ref, wo_ref, g_ref, b_ref, wr_ref, br_ref,
                            x2_ref, slab_ref, idx_ref, gate_ref, *, n_experts):
    x = x_ref[...]
    tq = x.shape[0]
    mw = MEM_HEADS * MEM_HEAD_DIM
    q = jnp.dot(x.astype(BF16), wq_ref[...], preferred_element_type=F32).astype(BF16)
    heads = []
    for h in range(MEM_HEADS):
        sl = slice(h * MEM_HEAD_DIM, (h + 1) * MEM_HEAD_DIM)
        s = _dot_nt(q[:, sl], kv_ref[:, sl]) * (MEM_HEAD_DIM ** -0.5)
        m = jnp.max(s, axis=-1, keepdims=True)
        e = jnp.exp(s - m)
        l = jnp.sum(e, axis=-1, keepdims=True)
        o = jnp.dot(e.astype(BF16), kv_ref[:, mw + h * MEM_HEAD_DIM:mw + (h + 1) * MEM_HEAD_DIM],
                    preferred_element_type=F32) / l
        heads.append(o.astype(BF16))
    o = jnp.concatenate(heads, axis=1)
    y = jnp.dot(o, wo_ref[...], preferred_element_type=F32)
    x2 = _layer_norm_rows(DEEPNORM_ALPHA * x + y, g_ref[...], b_ref[...])
    x2_ref[...] = x2
    for c in range(SLAB):
        slab_ref[pl.ds(c, tq, stride=SLAB), :] = x2[:, c * LANES:(c + 1) * LANES]

    xh, xl = _split_bf16(x2)
    wh, wl = _split_bf16(wr_ref[...])
    logits = (jnp.dot(xh, wh, preferred_element_type=F32) + jnp.dot(xh, wl, preferred_element_type=F32)
              + jnp.dot(xl, wh, preferred_element_type=F32)) + br_ref[...]
    lane = lax.broadcasted_iota(jnp.int32, (tq, LANES), 1)
    g = jnp.where(lane < n_experts, logits, NEG_INF)
    idx_out = jnp.zeros((tq, LANES), jnp.int32)
    val_out = jnp.zeros((tq, LANES), F32)
    top = None
    den = jnp.zeros((tq, 1), F32)
    for r in range(TOPK):
        m = jnp.max(g, axis=-1, keepdims=True)
        idx = jnp.min(jnp.where(g == m, lane, LANES), axis=-1, keepdims=True)
        g = jnp.where(lane == idx, NEG_INF, g)
        top = m if top is None else top
        e = jnp.exp(m - top)
        den = den + e
        idx_out = jnp.where(lane == r, idx, idx_out)
        val_out = jnp.where(lane == r, e, val_out)
    idx_ref[...] = idx_out
    gate_ref[...] = val_out / den


def _cross_ln_router(x3d, kv, wq_bf, wo_bf, g, b, w_router, b_router):
    bsz, s, d = x3d.shape
    m = kv.shape[1]
    n_experts = w_router.shape[1]
    tq = 512
    nq = s // tq
    wr = jnp.pad(w_router, ((0, 0), (0, LANES - n_experts)))
    br = jnp.pad(b_router, (0, LANES - n_experts)).reshape(1, LANES)
    full = lambda shp: pl.BlockSpec(shp, lambda bi, i: (0,) * len(shp))
    t = bsz * s
    return pl.pallas_call(
        functools.partial(_cross_ln_router_kernel, n_experts=n_experts),
        out_shape=(jax.ShapeDtypeStruct((t, d), F32), jax.ShapeDtypeStruct((t * SLAB, LANES), F32),
                   jax.ShapeDtypeStruct((t, LANES), jnp.int32), jax.ShapeDtypeStruct((t, LANES), F32)),
        grid=(bsz, nq),
        in_specs=[
            pl.BlockSpec((None, tq, d), lambda bi, i: (bi, i, 0)),
            pl.BlockSpec((None, m, kv.shape[2]), lambda bi, i: (bi, 0, 0)),
            full(wq_bf.shape), full(wo_bf.shape), full((1, d)), full((1, d)), full(wr.shape), full((1, LANES)),
        ],
        out_specs=(
            pl.BlockSpec((tq, d), lambda bi, i: (bi * nq + i, 0)),
            pl.BlockSpec((tq * SLAB, LANES), lambda bi, i: (bi * nq + i, 0)),
            pl.BlockSpec((tq, LANES), lambda bi, i: (bi * nq + i, 0)),
            pl.BlockSpec((tq, LANES), lambda bi, i: (bi * nq + i, 0)),
        ),
        compiler_params=_params(("arbitrary", "arbitrary")),
        name="cross_ln_router",
    )(x3d, kv, wq_bf, wo_bf, g.reshape(1, d), b.reshape(1, d), wr, br)


def _row_copy(src_hbm, src_row, dst, dst_row, sem):
    return pltpu.make_async_copy(src_hbm.at[pl.ds(src_row * SLAB, SLAB), :],
                                 dst.at[pl.ds(dst_row * PITCH, SLAB), :], sem)


def _issue_rows(idx_ref, n, src_hbm, dst, sem):
    def body(r2, carry):
        r = r2 * 2
        _row_copy(src_hbm, idx_ref[0, r], dst, r, sem).start(priority=0)
        _row_copy(src_hbm, idx_ref[0, r + 1], dst, r + 1, sem).start(priority=1)
        return carry

    lax.fori_loop(0, n // 2, body, 0)


def _wait_rows(n, src_hbm, dst, sem):
    pltpu.make_async_copy(src_hbm.at[pl.ds(0, n * SLAB), :], dst.at[pl.ds(0, n * SLAB), :], sem).wait()


def _gather_kernel(tok_ref, tok_next_ref, x_hbm, o_ref, slab0, slab1, sem, *, tm):
    i = pl.program_id(0)
    n = pl.num_programs(0)
    slabs = (slab0, slab1)

    @pl.when(i == 0)
    def _():
        _issue_rows(tok_ref, tm, x_hbm, slab0, sem.at[0])

    for s in range(2):
        @pl.when(i % 2 == s)
        def _(s=s):
            @pl.when(i + 1 < n)
            def _():
                _issue_rows(tok_next_ref, tm, x_hbm, slabs[1 - s], sem.at[1 - s])

            _wait_rows(tm, x_hbm, slabs[s], sem.at[s])
            for c in range(SLAB):
                o_ref[:, c * LANES:(c + 1) * LANES] = slabs[s][pl.ds(c, tm, stride=PITCH), :].astype(o_ref.dtype)


def _dispatch_gather(row_tok, x_slab, n_rows, tm, d):
    n_tiles = n_rows // tm
    tok3 = row_tok.reshape(n_tiles, 1, tm)
    return pl.pallas_call(
        functools.partial(_gather_kernel, tm=tm),
        out_shape=jax.ShapeDtypeStruct((n_rows, d), BF16),
        grid=(n_tiles,),
        in_specs=[pl.BlockSpec((None, 1, tm), lambda i: (i, 0, 0), memory_space=pltpu.SMEM),
                  pl.BlockSpec((None, 1, tm), lambda i: (jnp.minimum(i + 1, n_tiles - 1), 0, 0),
                               memory_space=pltpu.SMEM),
                  pl.BlockSpec(memory_space=pl.ANY)],
        out_specs=pl.BlockSpec((tm, d), lambda i: (i, 0)),
        scratch_shapes=[pltpu.VMEM((tm * PITCH, LANES), F32), pltpu.VMEM((tm * PITCH, LANES), F32),
                        pltpu.SemaphoreType.DMA((2,))],
        compiler_params=_params(("arbitrary",)),
        name="moe_dispatch",
    )(tok3, tok3, x_slab)


def _expert_kernel(te_ref, tv_ref, x_ref, wg_ref, wu_ref, bg_ref, bu_ref, wd_ref, bd_ref, o_ref, acc_ref, *, tm):
    i = pl.program_id(0)
    f = pl.program_id(1)
    nf = pl.num_programs(1)

    @pl.when((tv_ref[i] == 0) & (f == nf - 1))
    def _():
        o_ref[...] = jnp.zeros_like(o_ref)

    @pl.when(tv_ref[i] > 0)
    def _():
        x = x_ref[...]
        gate = jnp.dot(x, wg_ref[...], preferred_element_type=F32) + bg_ref[...]
        up = jnp.dot(x, wu_ref[...], preferred_element_type=F32) + bu_ref[...]
        gate = jnp.minimum(gate, SWIGLU_LIMIT)
        up = jnp.clip(up, -SWIGLU_LIMIT, SWIGLU_LIMIT)
        act = (up + 1.0) * (gate * jax.nn.sigmoid(SWIGLU_ALPHA * gate))
        part = jnp.dot(act.astype(BF16), wd_ref[...], preferred_element_type=F32)

        @pl.when(f == 0)
        def _():
            acc_ref[...] = part + bd_ref[...]

        @pl.when(f > 0)
        def _():
            acc_ref[...] += part

        @pl.when(f == nf - 1)
        def _():
            for c in range(SLAB):
                o_ref[pl.ds(c, tm, stride=SLAB), :] = acc_ref[:, c * LANES:(c + 1) * LANES]


def _expert_mlp(tile_expert, tile_valid, xs, wgu_bf, b_gu, wd_bf, b_down, layer, tm, tf):
    n_rows, d = xs.shape
    _, n_exp, _, two_ff = wgu_bf.shape
    d_ff = two_ff // 2
    nf = d_ff // tf
    wf = lambda i, f, tv: jnp.where(tv[i] > 0, f, nf - 1)
    return pl.pallas_call(
        functools.partial(_expert_kernel, tm=tm),
        out_shape=jax.ShapeDtypeStruct((n_rows * SLAB, LANES), F32),
        grid_spec=pltpu.PrefetchScalarGridSpec(
            num_scalar_prefetch=2,
            grid=(n_rows // tm, nf),
            in_specs=[
                pl.BlockSpec((tm, d), lambda i, f, te, tv: (i, 0)),
                pl.BlockSpec((None, None, d, tf), lambda i, f, te, tv: (layer, te[i], 0, wf(i, f, tv))),
                pl.BlockSpec((None, None, d, tf), lambda i, f, te, tv: (layer, te[i], 0, nf + wf(i, f, tv))),
                pl.BlockSpec((None, 1, tf), lambda i, f, te, tv: (te[i], 0, wf(i, f, tv))),
                pl.BlockSpec((None, 1, tf), lambda i, f, te, tv: (te[i], 0, nf + wf(i, f, tv))),
                pl.BlockSpec((None, None, tf, d), lambda i, f, te, tv: (layer, te[i], wf(i, f, tv), 0)),
                pl.BlockSpec((None, 1, d), lambda i, f, te, tv: (te[i], 0, 0)),
            ],
            out_specs=pl.BlockSpec((tm * SLAB, LANES), lambda i, f, te, tv: (i, 0)),
            scratch_shapes=[pltpu.VMEM((tm, d), F32)],
        ),
        compiler_params=_params(("arbitrary", "arbitrary")),
        name="moe_experts",
    )(tile_expert, tile_valid, xs, wgu_bf, wgu_bf, b_gu.reshape(n_exp, 1, two_ff), b_gu.reshape(n_exp, 1, two_ff),
      wd_bf, b_down.reshape(n_exp, 1, d))


def _combine_ln_kernel(pos_ref, pos_next_ref, y_hbm, gate_ref, x_ref, g_ref, b_ref, o_ref, slab0, slab1, y_scr, sem,
                       *, tt):
    i = pl.program_id(0)
    n = pl.num_programs(0)
    rows = tt * TOPK
    slabs = (slab0, slab1)

    @pl.when(i == 0)
    def _():
        _issue_rows(pos_ref, rows, y_hbm, slab0, sem.at[0])

    for s in range(2):
        @pl.when(i % 2 == s)
        def _(s=s):
            @pl.when(i + 1 < n)
            def _():
                _issue_rows(pos_next_ref, rows, y_hbm, slabs[1 - s], sem.at[1 - s])

            _wait_rows(rows, y_hbm, slabs[s], sem.at[s])
            gates = gate_ref[...]
            gk = [jnp.broadcast_to(gates[:, k:k + 1], (tt, LANES)) for k in range(TOPK)]
            for c in range(SLAB):
                acc = gk[0] * slabs[s][pl.ds(c, tt, stride=PITCH), :]
                for k in range(1, TOPK):
                    acc += gk[k] * slabs[s][pl.ds(k * tt * PITCH + c, tt, stride=PITCH), :]
                y_scr[:, c * LANES:(c + 1) * LANES] = acc

    o_ref[...] = _layer_norm_rows(DEEPNORM_ALPHA * x_ref[...] + y_scr[...], g_ref[...], b_ref[...])


def _combine_ln(pos, y_slab, gates, x2d, g, b):
    t, d = x2d.shape
    tt = 256
    n_tiles = t // tt
    pos3 = pos.reshape(n_tiles, tt, TOPK).transpose(0, 2, 1).reshape(n_tiles, 1, TOPK * tt)
    stage = pltpu.VMEM((tt * TOPK * PITCH, LANES), F32)
    return pl.pallas_call(
        functools.partial(_combine_ln_kernel, tt=tt),
        out_shape=jax.ShapeDtypeStruct((t, d), F32),
        grid=(n_tiles,),
        in_specs=[
            pl.BlockSpec((None, 1, tt * TOPK), lambda i: (i, 0, 0), memory_space=pltpu.SMEM),
            pl.BlockSpec((None, 1, tt * TOPK), lambda i: (jnp.minimum(i + 1, n_tiles - 1), 0, 0),
                         memory_space=pltpu.SMEM),
            pl.BlockSpec(memory_space=pl.ANY),
            pl.BlockSpec((tt, LANES), lambda i: (i, 0)),
            pl.BlockSpec((tt, d), lambda i: (i, 0)),
            pl.BlockSpec((1, d), lambda i: (0, 0)),
            pl.BlockSpec((1, d), lambda i: (0, 0)),
        ],
        out_specs=pl.BlockSpec((tt, d), lambda i: (i, 0)),
        scratch_shapes=[stage, stage, pltpu.VMEM((tt, d), F32), pltpu.SemaphoreType.DMA((2,))],
        compiler_params=_params(("arbitrary",)),
        name="moe_combine_ln",
    )(pos3, pos3, y_slab, gates, x2d, g.reshape(1, d), b.reshape(1, d))


def _routing_tables(top_idx, n_experts, tm):
    t = top_idx.shape[0]
    n_assign = t * TOPK
    e_flat = top_idx.reshape(-1)
    onehot = (e_flat[:, None] == jnp.arange(n_experts, dtype=jnp.int32)[None, :]).astype(jnp.int32)
    csum = jnp.cumsum(onehot, axis=0)
    counts = csum[-1]
    rank = jnp.sum((csum - onehot) * onehot, axis=1)
    padded = (counts + tm - 1) // tm * tm
    p_end = jnp.cumsum(padded)
    p_start = p_end - padded
    start = jnp.cumsum(counts) - counts
    pos = (p_start[e_flat] + rank).astype(jnp.int32)
    n_tiles = (n_assign + n_experts * (tm - 1)) // tm
    n_rows = n_tiles * tm
    order = jnp.argsort(e_flat)
    tile_expert = jnp.minimum(jnp.searchsorted(p_end, jnp.arange(n_tiles) * tm, side="right"),
                              n_experts - 1).astype(jnp.int32)
    tile_valid = (jnp.arange(n_tiles) * tm < p_end[-1]).astype(jnp.int32)
    rows = jnp.arange(n_rows)
    row_e = jnp.repeat(tile_expert, tm)
    off = rows - p_start[row_e]
    real = (off < counts[row_e]) & (rows < p_end[-1])
    src = jnp.clip(start[row_e] + off, 0, n_assign - 1)
    row_tok = jnp.where(real, order[src] // TOPK, 0).astype(jnp.int32)
    return row_tok, pos, tile_expert, tile_valid, n_rows


MOE_TM = 512
MOE_TF = 1024


def kernel(x, mem, w_in, b_in, attn_sinks, w_out, ln1_g, ln1_b, w_mem_q, w_mem_k, w_mem_v, w_mem_o, ln2_g, ln2_b,
           w_router, b_router, w_gate_up, b_gate_up, w_down, b_down, ln3_g, ln3_b):
    bsz, s, d = x.shape
    t = bsz * s
    depth = w_in.shape[0]
    n_experts = w_router.shape[-1]
    mem2d = mem.reshape(bsz * mem.shape[1], d)
    x2d = x.reshape(t, d)
    wgu_bf = w_gate_up.astype(BF16)
    wd_bf = w_down.astype(BF16)
    tf = min(MOE_TF, w_down.shape[2])
    for i in range(depth):
        proj = _inproj(x2d, w_in[i].astype(BF16), b_in[i]).reshape(bsz, s, IN_WIDTH)
        oa = _swa_attention(proj, attn_sinks[i])
        ob = _moba_attention(proj)
        oc = _dilated_attention(proj)
        x1 = _outproj_ln(oa.reshape(t, A_Q_W), ob.reshape(t, B_W), oc.reshape(t, C_W), w_out[i].astype(BF16), x2d,
                         ln1_g[i], ln1_b[i])
        wkv = jnp.concatenate([w_mem_k[i], w_mem_v[i]], axis=1).astype(BF16)
        kv = _mem_kv(mem2d, wkv).reshape(bsz, mem.shape[1], wkv.shape[1])
        x2, x2_slab, top_idx, gates = _cross_ln_router(x1.reshape(bsz, s, d), kv, w_mem_q[i].astype(BF16),
                                                       w_mem_o[i].astype(BF16), ln2_g[i], ln2_b[i],
                                                       w_router[i], b_router[i])
        row_tok, pos, tile_expert, tile_valid, n_rows = _routing_tables(top_idx[:, :TOPK], n_experts, MOE_TM)
        xs = _dispatch_gather(row_tok, x2_slab, n_rows, MOE_TM, d)
        y_slab = _expert_mlp(tile_expert, tile_valid, xs, wgu_bf, b_gate_up[i], wd_bf, b_down[i], i, MOE_TM, tf)
        x2d = _combine_ln(pos, y_slab, gates, x2, ln3_g[i], ln3_b[i])
    return x2d.reshape(bsz, s, d)
```

```python
import functools

import numpy as np
import jax
import jax.numpy as jnp
from jax import lax
from jax.experimental import pallas as pl
from jax.experimental.pallas import tpu as pltpu

F32 = jnp.float32
BF16 = jnp.bfloat16
NEG_INF = float("-inf")
MASK_VALUE = -1e30

LANES = 128
SUBLANES = 8
VMEM_LIMIT_BYTES = 56 * 1024 * 1024

HEAD_DIM = 64
A_Q_HEADS = 16
A_KV_HEADS = 2
A_GROUP = A_Q_HEADS // A_KV_HEADS
A_WINDOW = 128
B_HEADS = 8
MOBA_BLOCK = 256
MOBA_TOPK = 3
C_HEADS = 8
DILATED_PAIRS = ((128, 1), (512, 4), (2048, 16))
BAND = 128
A_Q_W = A_Q_HEADS * HEAD_DIM
A_KV_W = A_KV_HEADS * HEAD_DIM
B_W = B_HEADS * HEAD_DIM
C_W = C_HEADS * HEAD_DIM
IN_WIDTH = A_Q_W + 2 * A_KV_W + 3 * B_W + 3 * C_W
COL_AK = A_Q_W // LANES
COL_AV = COL_AK + A_KV_W // LANES
COL_BQ = COL_AV + A_KV_W // LANES
COL_BK = COL_BQ + B_W // LANES
COL_BV = COL_BK + B_W // LANES
COL_CQ = COL_BV + B_W // LANES
COL_CK = COL_CQ + C_W // LANES
COL_CV = COL_CK + C_W // LANES
MEM_HEADS = 4
MEM_HEAD_DIM = 128
TOPK = 4
SWIGLU_LIMIT = 7.0
SWIGLU_ALPHA = 1.702
LN_EPS = 1e-5
DEPTH = 2
DEEPNORM_ALPHA = (2 * DEPTH) ** 0.25
SLAB = 16
PITCH = 24


def _alibi_slopes(n):
    return [float(2.0 ** (-8.0 * (i + 1) / n)) for i in range(n)]


def _params(sem=None):
    return pltpu.CompilerParams(dimension_semantics=sem, vmem_limit_bytes=VMEM_LIMIT_BYTES)


def _layer_norm_rows(y, g, b):
    mu = jnp.mean(y, axis=-1, keepdims=True)
    d = y - mu
    var = jnp.mean(d * d, axis=-1, keepdims=True)
    return d * lax.rsqrt(var + LN_EPS) * g + b


def _dot_nt(a, b):
    return lax.dot_general(a, b, (((1,), (1,)), ((), ())), preferred_element_type=F32)


def _inproj_kernel(x_ref, w_ref, b_ref, o_ref):
    acc = jnp.dot(x_ref[...].astype(BF16), w_ref[...], preferred_element_type=F32)
    o_ref[...] = acc + b_ref[...]


def _inproj(x2d, w_bf, b):
    t, d = x2d.shape
    n = w_bf.shape[1]
    tm, tn = 512, n // 2
    return pl.pallas_call(
        _inproj_kernel,
        out_shape=jax.ShapeDtypeStruct((t, n), F32),
        grid=(n // tn, t // tm),
        in_specs=[
            pl.BlockSpec((tm, d), lambda j, i: (i, 0)),
            pl.BlockSpec((d, tn), lambda j, i: (0, j)),
            pl.BlockSpec((1, tn), lambda j, i: (0, j)),
        ],
        out_specs=pl.BlockSpec((tm, tn), lambda j, i: (i, j)),
        compiler_params=_params(("arbitrary", "arbitrary")),
        name="inproj",
    )(x2d, w_bf, b.reshape(1, n))


def _swa_kernel(sink_ref, q_ref, kp_ref, kc_ref, vp_ref, vc_ref, o_ref, bias_ref):
    first_step = (pl.program_id(0) == 0) & (pl.program_id(1) == 0)
    i = pl.program_id(1)
    ki = lax.broadcasted_iota(jnp.int32, (2 * BAND, BAND), 0)
    qi = lax.broadcasted_iota(jnp.int32, (2 * BAND, BAND), 1)
    slopes = _alibi_slopes(A_Q_HEADS)

    @pl.when(first_step)
    def _():
        dist = BAND + qi - ki
        in_band = (dist >= 0) & (dist <= A_WINDOW - 1)
        distf = dist.astype(F32)
        for h in range(A_Q_HEADS):
            bias_ref[h] = jnp.where(in_band, -slopes[h] * distf, MASK_VALUE)

    fix = jnp.where((i > 0) | (ki >= BAND), 0.0, MASK_VALUE)
    q = q_ref[0]
    k2 = jnp.concatenate([kp_ref[0], kc_ref[0]], axis=0).astype(BF16)
    v2t = jnp.concatenate([vp_ref[0], vc_ref[0]], axis=0).T
    row_v = lax.broadcasted_iota(jnp.int32, (HEAD_DIM, 2 * BAND), 0)
    ones_row = jnp.where(row_v == 0, 1.0, 0.0)
    ss, sinks = [], []
    for hk in range(A_KV_HEADS):
        heads = range(hk * A_GROUP, (hk + 1) * A_GROUP)
        kv = slice(hk * HEAD_DIM, (hk + 1) * HEAD_DIM)
        q_grp = jnp.concatenate([q[:, h * HEAD_DIM:(h + 1) * HEAD_DIM] for h in heads], axis=0)
        q_grp = (q_grp * (HEAD_DIM ** -0.5)).astype(BF16)
        s = _dot_nt(k2[:, kv], q_grp)
        ss.append(s + jnp.concatenate([bias_ref[h] + fix for h in heads], axis=1))
        sinks.append(jnp.concatenate([jnp.full((1, BAND), sink_ref[h], F32) for h in heads], axis=1))
    ms = [jnp.maximum(jnp.max(s, axis=0, keepdims=True), sk) for s, sk in zip(ss, sinks)]
    es = [jnp.exp(s - m).astype(BF16) for s, m in zip(ss, ms)]
    for hk in range(A_KV_HEADS):
        kv = slice(hk * HEAD_DIM, (hk + 1) * HEAD_DIM)
        vaug = jnp.concatenate([v2t[kv, :], ones_row], axis=0).astype(BF16)
        o = jnp.dot(vaug, es[hk], preferred_element_type=F32)
        den = o[HEAD_DIM:HEAD_DIM + 1, :] + jnp.exp(sinks[hk] - ms[hk])
        o = o[:HEAD_DIM, :] / den
        for g in range(0, A_GROUP, 2):
            pair = jnp.concatenate([o[:, g * BAND:(g + 1) * BAND], o[:, (g + 1) * BAND:(g + 2) * BAND]], axis=0)
            col = (hk * A_GROUP + g) * HEAD_DIM
            o_ref[0, :, col:col + 2 * HEAD_DIM] = pair.T.astype(o_ref.dtype)


def _swa_attention(proj, sinks):
    b, s, _ = proj.shape
    nblk = s // BAND
    prev = lambda i: jnp.maximum(i - 1, 0)
    return pl.pallas_call(
        _swa_kernel,
        out_shape=jax.ShapeDtypeStruct((b, s, A_Q_W), BF16),
        grid=(b, nblk),
        in_specs=[
            pl.BlockSpec(memory_space=pltpu.SMEM),
            pl.BlockSpec((1, BAND, A_Q_W), lambda bi, i: (bi, i, 0)),
            pl.BlockSpec((1, BAND, LANES), lambda bi, i: (bi, prev(i), COL_AK)),
            pl.BlockSpec((1, BAND, LANES), lambda bi, i: (bi, i, COL_AK)),
            pl.BlockSpec((1, BAND, LANES), lambda bi, i: (bi, prev(i), COL_AV)),
            pl.BlockSpec((1, BAND, LANES), lambda bi, i: (bi, i, COL_AV)),
        ],
        out_specs=pl.BlockSpec((1, BAND, A_Q_W), lambda bi, i: (bi, i, 0)),
        scratch_shapes=[pltpu.VMEM((A_Q_HEADS, 2 * BAND, BAND), F32)],
        compiler_params=_params(("arbitrary", "arbitrary")),
        name="swa_attention",
    )(sinks, proj, proj, proj, proj, proj)


PEN_COLS = 16


def _moba_kernel(slope_ref, q_ref, k_ref, v_ref, o_ref, kmean_ref, kaug_ref, vaug_ref, *, nb):
    hp = pl.program_id(1)
    j = pl.program_id(2)
    blk = MOBA_BLOCK
    nh = LANES // HEAD_DIM

    @pl.when(j == 0)
    def _():
        kmean_ref[...] = jnp.zeros_like(kmean_ref)
        lane_k = lax.broadcasted_iota(jnp.int32, (blk, LANES), 1)
        row_v = lax.broadcasted_iota(jnp.int32, (HEAD_DIM, blk), 0)
        ones_row = jnp.where(row_v == 0, 1.0, 0.0)
        for n in range(nb):
            kblk = k_ref[n * blk:(n + 1) * blk, :]
            kmean_ref[n:n + 1, :] = jnp.sum(kblk, axis=0, keepdims=True) * (1.0 / blk)
            onehot = jnp.where((lane_k == HEAD_DIM + n) | (lane_k == HEAD_DIM + PEN_COLS + n), 1.0, 0.0)
            vt = v_ref[n * blk:(n + 1) * blk, :].T
            for hh in range(nh):
                sl = slice(hh * HEAD_DIM, (hh + 1) * HEAD_DIM)
                k_low = kblk if hh == 0 else pltpu.roll(kblk, HEAD_DIM, 1)
                kaug_ref[hh, n] = jnp.where(lane_k < HEAD_DIM, k_low, onehot).astype(BF16)
                vaug_ref[hh, n] = jnp.concatenate([vt[sl, :], ones_row], axis=0).astype(BF16)

    ki = lax.broadcasted_iota(jnp.int32, (blk, blk), 0)
    qi = lax.broadcasted_iota(jnp.int32, (blk, blk), 1)
    rel = (qi - ki).astype(F32)
    causal = qi >= ki
    blk_id = lax.broadcasted_iota(jnp.int32, (PEN_COLS, blk), 0)
    qt_both = q_ref[...].T
    head_sl = [slice(hh * HEAD_DIM, (hh + 1) * HEAD_DIM) for hh in range(nh)]
    gates = [lax.dot_general(kmean_ref[:PEN_COLS, sl], q_ref[:, sl], (((1,), (1,)), ((), ())),
                             precision=lax.Precision.HIGHEST, preferred_element_type=F32) for sl in head_sl]
    qt_aug, bias = [], []
    for hh in range(nh):
        sl = head_sl[hh]
        slope = slope_ref[hp * nh + hh]
        g = jnp.where(blk_id < j, gates[hh], NEG_INF)
        sel = jnp.zeros((PEN_COLS, blk), jnp.bool_)
        for _ in range(MOBA_TOPK):
            m = jnp.max(g, axis=0, keepdims=True)
            idx = jnp.min(jnp.where(g == m, blk_id, PEN_COLS), axis=0, keepdims=True)
            hit = blk_id == idx
            sel = sel | (hit & (m > NEG_INF))
            g = jnp.where(hit, NEG_INF, g)
        block_off = -slope * ((j - blk_id) * blk).astype(F32)
        pen = jnp.where(blk_id == j, 0.0, jnp.where(sel, block_off, MASK_VALUE))
        pen_hi = pen.astype(BF16)
        pen_lo = (pen - pen_hi.astype(F32)).astype(BF16)
        qa = jnp.concatenate([(qt_both[sl, :] * (HEAD_DIM ** -0.5)).astype(BF16), pen_hi, pen_lo,
                              jnp.zeros((HEAD_DIM - 2 * PEN_COLS, blk), BF16)], axis=0)
        qt_aug.append(qa)
        bias.append(-slope * rel)

    def scores(n):
        return tuple(jnp.dot(kaug_ref[hh, n], qt_aug[hh], preferred_element_type=F32) + bias[hh] for hh in range(nh))

    s_own = [jnp.where(causal, s, MASK_VALUE) for s in scores(j)]
    s_first = scores(0)
    m0 = [jnp.max(s, axis=0, keepdims=True) for s in s_own]
    p0 = [jnp.exp(s - m).astype(BF16) for s, m in zip(s_own, m0)]
    o0 = [jnp.dot(vaug_ref[hh, j], p0[hh], preferred_element_type=F32) for hh in range(nh)]

    def body(n, carry):
        ms, os_, s_cur = carry
        s_next = scores(jnp.minimum(n + 1, nb - 1))
        new_m, ps = [], []
        for hh in range(nh):
            m_new = jnp.maximum(ms[hh], jnp.max(s_cur[hh], axis=0, keepdims=True))
            ps.append(jnp.exp(s_cur[hh] - m_new).astype(BF16))
            new_m.append(m_new)
        new_o = [jnp.exp(ms[hh] - new_m[hh]) * os_[hh] + jnp.dot(vaug_ref[hh, n], ps[hh], preferred_element_type=F32)
                 for hh in range(nh)]
        return tuple(new_m), tuple(new_o), s_next

    _, o_fin, _ = lax.fori_loop(0, j, body, (tuple(m0), tuple(o0), s_first))
    outs = [o[:HEAD_DIM, :] / o[HEAD_DIM:HEAD_DIM + 1, :] for o in o_fin]
    o_ref[...] = jnp.concatenate(outs, axis=0).T.astype(o_ref.dtype)


def _moba_attention(proj):
    b, s, _ = proj.shape
    nb = s // MOBA_BLOCK
    assert nb <= PEN_COLS
    npair = B_W // LANES
    nh = LANES // HEAD_DIM
    slopes = jnp.asarray(_alibi_slopes(B_HEADS), F32)
    return pl.pallas_call(
        functools.partial(_moba_kernel, nb=nb),
        out_shape=jax.ShapeDtypeStruct((b, s, B_W), BF16),
        grid=(b, npair, nb),
        in_specs=[
            pl.BlockSpec(memory_space=pltpu.SMEM),
            pl.BlockSpec((None, MOBA_BLOCK, LANES), lambda bi, hp, j: (bi, j, COL_BQ + hp)),
            pl.BlockSpec((None, s, LANES), lambda bi, hp, j: (bi, 0, COL_BK + hp)),
            pl.BlockSpec((None, s, LANES), lambda bi, hp, j: (bi, 0, COL_BV + hp)),
        ],
        out_specs=pl.BlockSpec((None, MOBA_BLOCK, LANES), lambda bi, hp, j: (bi, j, hp)),
        scratch_shapes=[pltpu.VMEM((LANES, LANES), F32),
                        pltpu.VMEM((nh, nb, MOBA_BLOCK, LANES), BF16),
                        pltpu.VMEM((nh, nb, LANES, MOBA_BLOCK), BF16)],
        compiler_params=_params(("arbitrary", "arbitrary", "arbitrary")),
        name="moba_attention",
    )(slopes, proj, proj, proj)


DIL_GROUP = 4


def _dilated_kernel(slope_ref, q_ref, k_ref, v_ref, o_ref, o0, o1, o2, e0, e1, e2, *, seq):
    hp = pl.program_id(1)
    o_scr = (o0, o1, o2)
    lse_scr = (e0, e1, e2)
    nh = LANES // HEAD_DIM
    ki = lax.broadcasted_iota(jnp.int32, (2 * BAND, BAND), 0)
    qi = lax.broadcasted_iota(jnp.int32, (2 * BAND, BAND), 1)
    dist = BAND + qi - ki
    distf = dist.astype(F32)
    first_fix = jnp.where(ki >= BAND, 0.0, MASK_VALUE)
    row_v = lax.broadcasted_iota(jnp.int32, (HEAD_DIM, 2 * BAND), 0)
    ones_row = jnp.where(row_v == 0, 1.0, 0.0)

    for p, (window, r) in enumerate(DILATED_PAIRS):
        nblk = seq // (r * BAND)
        in_band = (dist >= 0) & (dist <= window // r)
        biases = [jnp.where(in_band, -(slope_ref[hp * nh + hh] * float(r)) * distf, MASK_VALUE) for hh in range(nh)]

        def group(t, carry, p=p, r=r, biases=biases):
            rows, vts, ss = [], [], []
            for gidx in range(DIL_GROUP):
                tb = t * DIL_GROUP + gidx
                c = tb % r
                i = tb // r
                cur = c + i * (BAND * r)
                prv = c + jnp.maximum(i - 1, 0) * (BAND * r)
                rows_cur = pl.ds(cur, BAND, stride=r) if r > 1 else pl.ds(cur, BAND)
                rows_prv = pl.ds(prv, BAND, stride=r) if r > 1 else pl.ds(prv, BAND)
                q = (q_ref[rows_cur, :] * (HEAD_DIM ** -0.5)).astype(BF16)
                k2 = jnp.concatenate([k_ref[rows_prv, :], k_ref[rows_cur, :]], axis=0).astype(BF16)
                fix = jnp.where(i > 0, 0.0, first_fix)
                for hh in range(nh):
                    sl = slice(hh * HEAD_DIM, (hh + 1) * HEAD_DIM)
                    ss.append(_dot_nt(k2[:, sl], q[:, sl]) + biases[hh] + fix)
                rows.append(rows_cur)
                vts.append(jnp.concatenate([v_ref[rows_prv, :], v_ref[rows_cur, :]], axis=0).T)
            ms = [jnp.max(sc, axis=0, keepdims=True) for sc in ss]
            es = [jnp.exp(sc - m).astype(BF16) for sc, m in zip(ss, ms)]
            os_ = []
            for gidx in range(DIL_GROUP):
                for hh in range(nh):
                    sl = slice(hh * HEAD_DIM, (hh + 1) * HEAD_DIM)
                    vaug = jnp.concatenate([vts[gidx][sl, :], ones_row], axis=0).astype(BF16)
                    os_.append(jnp.dot(vaug, es[gidx * nh + hh], preferred_element_type=F32))
            for gidx in range(DIL_GROUP):
                outs, lses = [], []
                for hh in range(nh):
                    o = os_[gidx * nh + hh]
                    l = o[HEAD_DIM:HEAD_DIM + 1, :]
                    outs.append(o[:HEAD_DIM, :] / l)
                    lses.append(jnp.broadcast_to(ms[gidx * nh + hh] + jnp.log(l), (HEAD_DIM, BAND)))
                o_scr[p][rows[gidx], :] = jnp.concatenate(outs, axis=0).T
                lse_scr[p][rows[gidx], :] = jnp.concatenate(lses, axis=0).T
            return carry

        lax.fori_loop(0, r * nblk // DIL_GROUP, group, 0)

    chunk = 512

    def combine(t, carry):
        rows = pl.ds(pl.multiple_of(t * chunk, chunk), chunk)
        ls = [e[rows, :] for e in lse_scr]
        m = jnp.maximum(jnp.maximum(ls[0], ls[1]), ls[2])
        ws = [jnp.exp(l - m) for l in ls]
        den = ws[0] + ws[1] + ws[2]
        num = ws[0] * o_scr[0][rows, :] + ws[1] * o_scr[1][rows, :] + ws[2] * o_scr[2][rows, :]
        o_ref[rows, :] = (num / den).astype(o_ref.dtype)
        return carry

    lax.fori_loop(0, seq // chunk, combine, 0)


def _dilated_attention(proj):
    b, s, _ = proj.shape
    npair = C_W // LANES
    slopes = jnp.asarray(_alibi_slopes(C_HEADS), F32)
    spec = lambda col: pl.BlockSpec((None, s, LANES), lambda bi, hp: (bi, 0, col + hp))
    return pl.pallas_call(
        functools.partial(_dilated_kernel, seq=s),
        out_shape=jax.ShapeDtypeStruct((b, s, C_W), BF16),
        grid=(b, npair),
        in_specs=[pl.BlockSpec(memory_space=pltpu.SMEM), spec(COL_CQ), spec(COL_CK), spec(COL_CV)],
        out_specs=pl.BlockSpec((None, s, LANES), lambda bi, hp: (bi, 0, hp)),
        scratch_shapes=[pltpu.VMEM((s, LANES), F32) for _ in range(6)],
        compiler_params=_params(("arbitrary", "arbitrary")),
        name="dilated_attention",
    )(slopes, proj, proj, proj)


def _outproj_ln_kernel(oa_ref, ob_ref, oc_ref, w_ref, x_ref, g_ref, b_ref, o_ref):
    acc = jnp.dot(oa_ref[...], w_ref[0:A_Q_W, :], preferred_element_type=F32)
    acc += jnp.dot(ob_ref[...], w_ref[A_Q_W:A_Q_W + B_W, :], preferred_element_type=F32)
    acc += jnp.dot(oc_ref[...], w_ref[A_Q_W + B_W:, :], preferred_element_type=F32)
    o_ref[...] = _layer_norm_rows(DEEPNORM_ALPHA * x_ref[...] + acc, g_ref[...], b_ref[...])


def _outproj_ln(oa, ob, oc, w_bf, x2d, g, b):
    t, d = x2d.shape
    tm = 512
    row = lambda w: pl.BlockSpec((tm, w), lambda i: (i, 0))
    full = lambda shp: pl.BlockSpec(shp, lambda i: (0, 0))
    return pl.pallas_call(
        _outproj_ln_kernel,
        out_shape=jax.ShapeDtypeStruct((t, d), F32),
        grid=(t // tm,),
        in_specs=[row(A_Q_W), row(B_W), row(C_W), full(w_bf.shape), row(d), full((1, d)), full((1, d))],
        out_specs=row(d),
        compiler_params=_params(("arbitrary",)),
        name="outproj_ln",
    )(oa, ob, oc, w_bf, x2d, g.reshape(1, d), b.reshape(1, d))


def _memkv_kernel(m_ref, w_ref, o_ref):
    o_ref[...] = jnp.dot(m_ref[...].astype(BF16), w_ref[...], preferred_element_type=F32).astype(o_ref.dtype)


def _mem_kv(mem2d, wkv_bf):
    t, d = mem2d.shape
    n = wkv_bf.shape[1]
    tm = min(512, t)
    return pl.pallas_call(
        _memkv_kernel,
        out_shape=jax.ShapeDtypeStruct((t, n), BF16),
        grid=(t // tm,),
        in_specs=[pl.BlockSpec((tm, d), lambda i: (i, 0)), pl.BlockSpec((d, n), lambda i: (0, 0))],
        out_specs=pl.BlockSpec((tm, n), lambda i: (i, 0)),
        compiler_params=_params(("arbitrary",)),
        name="mem_kv",
    )(mem2d, wkv_bf)


def _split_bf16(a):
    hi = a.astype(BF16)
    lo = (a - hi.astype(F32)).astype(BF16)
    return hi, lo


def _cross_ln_router_kernel(x_ref, kv_ref, wq_ref, wo_ref, g_ref, b_ref, wr_ref, br_ref,
                            x2_ref, slab_ref, idx_ref, gate_ref, *, n_experts):
    x = x_ref[...]
    tq = x.shape[0]
    mw = MEM_HEADS * MEM_HEAD_DIM
    q = jnp.dot(x.astype(BF16), wq_ref[...], preferred_element_type=F32).astype(BF16)
    heads = []
    for h in range(MEM_HEADS):
        sl = slice(h * MEM_HEAD_DIM, (h + 1) * MEM_HEAD_DIM)
        s = _dot_nt(q[:, sl], kv_ref[:, sl]) * (MEM_HEAD_DIM ** -0.5)
        m = jnp.max(s, axis=-1, keepdims=True)
        e = jnp.exp(s - m)
        l = jnp.sum(e, axis=-1, keepdims=True)
        o = jnp.dot(e.astype(BF16), kv_ref[:, mw + h * MEM_HEAD_DIM:mw + (h + 1) * MEM_HEAD_DIM],
                    preferred_element_type=F32) / l
        heads.append(o.astype(BF16))
    o = jnp.concatenate(heads, axis=1)
    y = jnp.dot(o, wo_ref[...], preferred_element_type=F32)
    x2 = _layer_norm_rows(DEEPNORM_ALPHA * x + y, g_ref[...], b_ref[...])
    x2_ref[...] = x2
    for c in range(SLAB):
        slab_ref[pl.ds(c, tq, stride=SLAB), :] = x2[:, c * LANES:(c + 1) * LANES]

    xh, xl = _split_bf16(x2)
    wh, wl = _split_bf16(wr_ref[...])
    logits = (jnp.dot(xh, wh, preferred_element_type=F32) + jnp.dot(xh, wl, preferred_element_type=F32)
              + jnp.dot(xl, wh, preferred_element_type=F32)) + br_ref[...]
    lane = lax.broadcasted_iota(jnp.int32, (tq, LANES), 1)
    g = jnp.where(lane < n_experts, logits, NEG_INF)
    idx_out = jnp.zeros((tq, LANES), jnp.int32)
    val_out = jnp.zeros((tq, LANES), F32)
    top = None
    den = jnp.zeros((tq, 1), F32)
    for r in range(TOPK):
        m = jnp.max(g, axis=-1, keepdims=True)
        idx = jnp.min(jnp.where(g == m, lane, LANES), axis=-1, keepdims=True)
        g = jnp.where(lane == idx, NEG_INF, g)
        top = m if top is None else top
        e = jnp.exp(m - top)
        den = den + e
        idx_out = jnp.where(lane == r, idx, idx_out)
        val_out = jnp.where(lane == r, e, val_out)
    idx_ref[...] = idx_out
    gate_ref[...] = val_out / den


def _cross_ln_router(x3d, kv, wq_bf, wo_bf, g, b, w_router, b_router):
    bsz, s, d = x3d.shape
    m = kv.shape[1]
    n_experts = w_router.shape[1]
    tq = 512
    nq = s // tq
    wr = jnp.pad(w_router, ((0, 0), (0, LANES - n_experts)))
    br = jnp.pad(b_router, (0, LANES - n_experts)).reshape(1, LANES)
    full = lambda shp: pl.BlockSpec(shp, lambda bi, i: (0,) * len(shp))
    t = bsz * s
    return pl.pallas_call(
        functools.partial(_cross_ln_router_kernel, n_experts=n_experts),
        out_shape=(jax.ShapeDtypeStruct((t, d), F32), jax.ShapeDtypeStruct((t * SLAB, LANES), F32),
                   jax.ShapeDtypeStruct((t, LANES), jnp.int32), jax.ShapeDtypeStruct((t, LANES), F32)),
        grid=(bsz, nq),
        in_specs=[
            pl.BlockSpec((None, tq, d), lambda bi, i: (bi, i, 0)),
            pl.BlockSpec((None, m, kv.shape[2]), lambda bi, i: (bi, 0, 0)),
            full(wq_bf.shape), full(wo_bf.shape), full((1, d)), full((1, d)), full(wr.shape), full((1, LANES)),
        ],
        out_specs=(
            pl.BlockSpec((tq, d), lambda bi, i: (bi * nq + i, 0)),
            pl.BlockSpec((tq * SLAB, LANES), lambda bi, i: (bi * nq + i, 0)),
            pl.BlockSpec((tq, LANES), lambda bi, i: (bi * nq + i, 0)),
            pl.BlockSpec((tq, LANES), lambda bi, i: (bi * nq + i, 0)),
        ),
        compiler_params=_params(("arbitrary", "arbitrary")),
        name="cross_ln_router",
    )(x3d, kv, wq_bf, wo_bf, g.reshape(1, d), b.reshape(1, d), wr, br)


def _row_copy(src_hbm, src_row, dst, dst_row, sem):
    return pltpu.make_async_copy(src_hbm.at[pl.ds(src_row * SLAB, SLAB), :],
                                 dst.at[pl.ds(dst_row * PITCH, SLAB), :], sem)


ISSUE_UNROLL = 8


def _issue_rows(idx_ref, n, src_hbm, dst, sem):
    def body(g, carry):
        r0 = g * ISSUE_UNROLL
        for u in range(ISSUE_UNROLL):
            _row_copy(src_hbm, idx_ref[0, r0 + u], dst, r0 + u, sem).start(priority=u % 2)
        return carry

    lax.fori_loop(0, n // ISSUE_UNROLL, body, 0)


def _wait_rows(n, src_hbm, dst, sem):
    pltpu.make_async_copy(src_hbm.at[pl.ds(0, n * SLAB), :], dst.at[pl.ds(0, n * SLAB), :], sem).wait()


def _gather_ring(n_tiles, idx_refs, rows, src_hbm, slabs, sem, consume):
    depth = len(slabs)
    i = pl.program_id(0)

    @pl.when(i == 0)
    def _():
        for d in range(min(depth - 1, n_tiles)):
            _issue_rows(idx_refs[d], rows, src_hbm, slabs[d], sem.at[d])

    for s in range(depth):
        @pl.when(i % depth == s)
        def _(s=s):
            ahead = (s + depth - 1) % depth

            @pl.when(i + depth - 1 < n_tiles)
            def _():
                _issue_rows(idx_refs[depth - 1], rows, src_hbm, slabs[ahead], sem.at[ahead])

            _wait_rows(rows, src_hbm, slabs[s], sem.at[s])
            consume(slabs[s])


def _ring_index_specs(n_tiles, width, depth):
    return [pl.BlockSpec((None, 1, width), lambda i, d=d: (jnp.minimum(i + d, n_tiles - 1), 0, 0),
                         memory_space=pltpu.SMEM) for d in range(depth)]


DISPATCH_DEPTH = 3


def _gather_kernel(*refs, tm, n_tiles):
    tok_refs = refs[:DISPATCH_DEPTH]
    x_hbm, o_ref = refs[DISPATCH_DEPTH:DISPATCH_DEPTH + 2]
    slabs = refs[DISPATCH_DEPTH + 2:2 * DISPATCH_DEPTH + 2]
    sem = refs[2 * DISPATCH_DEPTH + 2]

    def relayout(slab):
        for c in range(SLAB):
            o_ref[:, c * LANES:(c + 1) * LANES] = slab[pl.ds(c, tm, stride=PITCH), :].astype(o_ref.dtype)

    _gather_ring(n_tiles, tok_refs, tm, x_hbm, slabs, sem, relayout)


def _dispatch_gather(row_tok, x_slab, n_rows, tm, d):
    n_tiles = n_rows // tm
    tok3 = row_tok.reshape(n_tiles, 1, tm)
    return pl.pallas_call(
        functools.partial(_gather_kernel, tm=tm, n_tiles=n_tiles),
        out_shape=jax.ShapeDtypeStruct((n_rows, d), BF16),
        grid=(n_tiles,),
        in_specs=_ring_index_specs(n_tiles, tm, DISPATCH_DEPTH) + [pl.BlockSpec(memory_space=pl.ANY)],
        out_specs=pl.BlockSpec((tm, d), lambda i: (i, 0)),
        scratch_shapes=[pltpu.VMEM((tm * PITCH, LANES), F32) for _ in range(DISPATCH_DEPTH)]
        + [pltpu.SemaphoreType.DMA((DISPATCH_DEPTH,))],
        compiler_params=_params(("arbitrary",)),
        name="moe_dispatch",
    )(*([tok3] * DISPATCH_DEPTH), x_slab)


def _expert_kernel(te_ref, tv_ref, x_ref, wg_ref, wu_ref, bg_ref, bu_ref, wd_ref, bd_ref, o_ref, acc_ref, *, tm):
    i = pl.program_id(0)
    f = pl.program_id(1)
    nf = pl.num_programs(1)

    @pl.when((tv_ref[i] == 0) & (f == nf - 1))
    def _():
        o_ref[...] = jnp.zeros_like(o_ref)

    @pl.when(tv_ref[i] > 0)
    def _():
        @pl.when(f == 0)
        def _():
            acc_ref[...] = jnp.broadcast_to(bd_ref[...], acc_ref.shape)

        x = x_ref[...]
        gate = jnp.dot(x, wg_ref[...], preferred_element_type=F32) + bg_ref[...]
        up = jnp.dot(x, wu_ref[...], preferred_element_type=F32) + bu_ref[...]
        gate = jnp.minimum(gate, SWIGLU_LIMIT)
        up = jnp.clip(up, -SWIGLU_LIMIT, SWIGLU_LIMIT)
        act = (up + 1.0) * (gate * jax.nn.sigmoid(SWIGLU_ALPHA * gate))
        val = acc_ref[...] + jnp.dot(act.astype(BF16), wd_ref[...], preferred_element_type=F32)
        acc_ref[...] = val
        for c in range(SLAB):
            o_ref[pl.ds(c, tm, stride=SLAB), :] = val[:, c * LANES:(c + 1) * LANES]


def _expert_mlp(tile_expert, tile_valid, xs, wgu_bf, b_gu, wd_bf, b_down, layer, tm, tf):
    n_rows, d = xs.shape
    _, n_exp, _, two_ff = wgu_bf.shape
    d_ff = two_ff // 2
    nf = d_ff // tf
    wf = lambda i, f, tv: jnp.where(tv[i] > 0, f, nf - 1)
    return pl.pallas_call(
        functools.partial(_expert_kernel, tm=tm),
        out_shape=jax.ShapeDtypeStruct((n_rows * SLAB, LANES), F32),
        grid_spec=pltpu.PrefetchScalarGridSpec(
            num_scalar_prefetch=2,
            grid=(n_rows // tm, nf),
            in_specs=[
                pl.BlockSpec((tm, d), lambda i, f, te, tv: (i, 0)),
                pl.BlockSpec((None, None, d, tf), lambda i, f, te, tv: (layer, te[i], 0, wf(i, f, tv))),
                pl.BlockSpec((None, None, d, tf), lambda i, f, te, tv: (layer, te[i], 0, nf + wf(i, f, tv))),
                pl.BlockSpec((None, 1, tf), lambda i, f, te, tv: (te[i], 0, wf(i, f, tv))),
                pl.BlockSpec((None, 1, tf), lambda i, f, te, tv: (te[i], 0, nf + wf(i, f, tv))),
                pl.BlockSpec((None, None, tf, d), lambda i, f, te, tv: (layer, te[i], wf(i, f, tv), 0)),
                pl.BlockSpec((None, 1, d), lambda i, f, te, tv: (te[i], 0, 0)),
            ],
            out_specs=pl.BlockSpec((tm * SLAB, LANES), lambda i, f, te, tv: (i, 0)),
            scratch_shapes=[pltpu.VMEM((tm, d), F32)],
        ),
        compiler_params=_params(("arbitrary", "arbitrary")),
        name="moe_experts",
    )(tile_expert, tile_valid, xs, wgu_bf, wgu_bf, b_gu.reshape(n_exp, 1, two_ff), b_gu.reshape(n_exp, 1, two_ff),
      wd_bf, b_down.reshape(n_exp, 1, d))


COMBINE_DEPTH = 2


def _combine_ln_kernel(*refs, tt, n_tiles):
    pos_refs = refs[:COMBINE_DEPTH]
    y_hbm, gate_ref, x_ref, g_ref, b_ref, o_ref = refs[COMBINE_DEPTH:COMBINE_DEPTH + 6]
    slabs = refs[COMBINE_DEPTH + 6:2 * COMBINE_DEPTH + 6]
    y_scr, sem = refs[2 * COMBINE_DEPTH + 6:]

    def weighted_sum(slab):
        gates = gate_ref[...]
        gk = [jnp.broadcast_to(gates[:, k:k + 1], (tt, LANES)) for k in range(TOPK)]
        for c in range(SLAB):
            acc = gk[0] * slab[pl.ds(c, tt, stride=PITCH), :]
            for k in range(1, TOPK):
                acc += gk[k] * slab[pl.ds(k * tt * PITCH + c, tt, stride=PITCH), :]
            y_scr[:, c * LANES:(c + 1) * LANES] = acc

    _gather_ring(n_tiles, pos_refs, tt * TOPK, y_hbm, slabs, sem, weighted_sum)
    o_ref[...] = _layer_norm_rows(DEEPNORM_ALPHA * x_ref[...] + y_scr[...], g_ref[...], b_ref[...])


def _combine_ln(pos, y_slab, gates, x2d, g, b):
    t, d = x2d.shape
    tt = 256
    n_tiles = t // tt
    pos3 = pos.reshape(n_tiles, tt, TOPK).transpose(0, 2, 1).reshape(n_tiles, 1, TOPK * tt)
    stage = pltpu.VMEM((tt * TOPK * PITCH, LANES), F32)
    return pl.pallas_call(
        functools.partial(_combine_ln_kernel, tt=tt, n_tiles=n_tiles),
        out_shape=jax.ShapeDtypeStruct((t, d), F32),
        grid=(n_tiles,),
        in_specs=_ring_index_specs(n_tiles, tt * TOPK, COMBINE_DEPTH) + [
            pl.BlockSpec(memory_space=pl.ANY),
            pl.BlockSpec((tt, LANES), lambda i: (i, 0)),
            pl.BlockSpec((tt, d), lambda i: (i, 0)),
            pl.BlockSpec((1, d), lambda i: (0, 0)),
            pl.BlockSpec((1, d), lambda i: (0, 0)),
        ],
        out_specs=pl.BlockSpec((tt, d), lambda i: (i, 0)),
        scratch_shapes=[stage] * COMBINE_DEPTH + [pltpu.VMEM((tt, d), F32),
                                                  pltpu.SemaphoreType.DMA((COMBINE_DEPTH,))],
        compiler_params=_params(("arbitrary",)),
        name="moe_combine_ln",
    )(*([pos3] * COMBINE_DEPTH), y_slab, gates, x2d, g.reshape(1, d), b.reshape(1, d))


def _routing_tables(top_idx, n_experts, tm):
    t = top_idx.shape[0]
    n_assign = t * TOPK
    e_flat = top_idx.reshape(-1)
    onehot = (e_flat[:, None] == jnp.arange(n_experts, dtype=jnp.int32)[None, :]).astype(jnp.int32)
    csum = jnp.cumsum(onehot, axis=0)
    counts = csum[-1]
    rank = jnp.sum((csum - onehot) * onehot, axis=1)
    padded = (counts + tm - 1) // tm * tm
    p_end = jnp.cumsum(padded)
    p_start = p_end - padded
    start = jnp.cumsum(counts) - counts
    pos = (p_start[e_flat] + rank).astype(jnp.int32)
    n_tiles = (n_assign + n_experts * (tm - 1)) // tm
    n_rows = n_tiles * tm
    order = jnp.argsort(e_flat)
    tile_expert = jnp.minimum(jnp.searchsorted(p_end, jnp.arange(n_tiles) * tm, side="right"),
                              n_experts - 1).astype(jnp.int32)
    tile_valid = (jnp.arange(n_tiles) * tm < p_end[-1]).astype(jnp.int32)
    rows = jnp.arange(n_rows)
    row_e = jnp.repeat(tile_expert, tm)
    off = rows - p_start[row_e]
    real = (off < counts[row_e]) & (rows < p_end[-1])
    src = jnp.clip(start[row_e] + off, 0, n_assign - 1)
    row_tok = jnp.where(real, order[src] // TOPK, 0).astype(jnp.int32)
    return row_tok, pos, tile_expert, tile_valid, n_rows


MOE_TM = 512
MOE_TF = 1024


def kernel(x, mem, w_in, b_in, attn_sinks, w_out, ln1_g, ln1_b, w_mem_q, w_mem_k, w_mem_v, w_mem_o, ln2_g, ln2_b,
           w_router, b_router, w_gate_up, b_gate_up, w_down, b_down, ln3_g, ln3_b):
    bsz, s, d = x.shape
    t = bsz * s
    depth = w_in.shape[0]
    n_experts = w_router.shape[-1]
    mem2d = mem.reshape(bsz * mem.shape[1], d)
    x2d = x.reshape(t, d)
    wgu_bf = w_gate_up.astype(BF16)
    wd_bf = w_down.astype(BF16)
    tf = min(MOE_TF, w_down.shape[2])
    for i in range(depth):
        proj = _inproj(x2d, w_in[i].astype(BF16), b_in[i]).reshape(bsz, s, IN_WIDTH)
        oa = _swa_attention(proj, attn_sinks[i])
        ob = _moba_attention(proj)
        oc = _dilated_attention(proj)
        x1 = _outproj_ln(oa.reshape(t, A_Q_W), ob.reshape(t, B_W), oc.reshape(t, C_W), w_out[i].astype(BF16), x2d,
                         ln1_g[i], ln1_b[i])
        wkv = jnp.concatenate([w_mem_k[i], w_mem_v[i]], axis=1).astype(BF16)
        kv = _mem_kv(mem2d, wkv).reshape(bsz, mem.shape[1], wkv.shape[1])
        x2, x2_slab, top_idx, gates = _cross_ln_router(x1.reshape(bsz, s, d), kv, w_mem_q[i].astype(BF16),
                                                       w_mem_o[i].astype(BF16), ln2_g[i], ln2_b[i],
                                                       w_router[i], b_router[i])
        row_tok, pos, tile_expert, tile_valid, n_rows = _routing_tables(top_idx[:, :TOPK], n_experts, MOE_TM)
        xs = _dispatch_gather(row_tok, x2_slab, n_rows, MOE_TM, d)
        y_slab = _expert_mlp(tile_expert, tile_valid, xs, wgu_bf, b_gate_up[i], wd_bf, b_down[i], i, MOE_TM, tf)
        x2d = _combine_ln(pos, y_slab, gates, x2, ln3_g[i], ln3_b[i])
    return x2d.reshape(bsz, s, d)
```

```python
import functools

import numpy as np
import jax
import jax.numpy as jnp
from jax import lax
from jax.experimental import pallas as pl
from jax.experimental.pallas import tpu as pltpu

F32 = jnp.float32
BF16 = jnp.bfloat16
NEG_INF = float("-inf")
MASK_VALUE = -1e30

LANES = 128
SUBLANES = 8
VMEM_LIMIT_BYTES = 56 * 1024 * 1024

HEAD_DIM = 64
A_Q_HEADS = 16
A_KV_HEADS = 2
A_GROUP = A_Q_HEADS // A_KV_HEADS
A_WINDOW = 128
B_HEADS = 8
MOBA_BLOCK = 256
MOBA_TOPK = 3
C_HEADS = 8
DILATED_PAIRS = ((128, 1), (512, 4), (2048, 16))
BAND = 128
A_Q_W = A_Q_HEADS * HEAD_DIM
A_KV_W = A_KV_HEADS * HEAD_DIM
B_W = B_HEADS * HEAD_DIM
C_W = C_HEADS * HEAD_DIM
IN_WIDTH = A_Q_W + 2 * A_KV_W + 3 * B_W + 3 * C_W
COL_AK = A_Q_W // LANES
COL_AV = COL_AK + A_KV_W // LANES
COL_BQ = COL_AV + A_KV_W // LANES
COL_BK = COL_BQ + B_W // LANES
COL_BV = COL_BK + B_W // LANES
COL_CQ = COL_BV + B_W // LANES
COL_CK = COL_CQ + C_W // LANES
COL_CV = COL_CK + C_W // LANES
MEM_HEADS = 4
MEM_HEAD_DIM = 128
TOPK = 4
SWIGLU_LIMIT = 7.0
SWIGLU_ALPHA = 1.702
LN_EPS = 1e-5
DEPTH = 2
DEEPNORM_ALPHA = (2 * DEPTH) ** 0.25
SLAB = 16
PITCH = 24


def _alibi_slopes(n):
    return [float(2.0 ** (-8.0 * (i + 1) / n)) for i in range(n)]


def _params(sem=None):
    return pltpu.CompilerParams(dimension_semantics=sem, vmem_limit_bytes=VMEM_LIMIT_BYTES)


def _layer_norm_rows(y, g, b):
    mu = jnp.mean(y, axis=-1, keepdims=True)
    d = y - mu
    var = jnp.mean(d * d, axis=-1, keepdims=True)
    return d * lax.rsqrt(var + LN_EPS) * g + b


def _dot_nt(a, b):
    return lax.dot_general(a, b, (((1,), (1,)), ((), ())), preferred_element_type=F32)


def _inproj_kernel(x_ref, w_ref, b_ref, o_ref):
    acc = jnp.dot(x_ref[...].astype(BF16), w_ref[...], preferred_element_type=F32)
    o_ref[...] = acc + b_ref[...]


def _inproj(x2d, w_bf, b):
    t, d = x2d.shape
    n = w_bf.shape[1]
    tm, tn = 512, n // 2
    return pl.pallas_call(
        _inproj_kernel,
        out_shape=jax.ShapeDtypeStruct((t, n), F32),
        grid=(n // tn, t // tm),
        in_specs=[
            pl.BlockSpec((tm, d), lambda j, i: (i, 0)),
            pl.BlockSpec((d, tn), lambda j, i: (0, j)),
            pl.BlockSpec((1, tn), lambda j, i: (0, j)),
        ],
        out_specs=pl.BlockSpec((tm, tn), lambda j, i: (i, j)),
        compiler_params=_params(("arbitrary", "arbitrary")),
        name="inproj",
    )(x2d, w_bf, b.reshape(1, n))


def _swa_kernel(sink_ref, q_ref, kp_ref, kc_ref, vp_ref, vc_ref, o_ref, bias_ref):
    first_step = (pl.program_id(0) == 0) & (pl.program_id(1) == 0)
    i = pl.program_id(1)
    ki = lax.broadcasted_iota(jnp.int32, (2 * BAND, BAND), 0)
    qi = lax.broadcasted_iota(jnp.int32, (2 * BAND, BAND), 1)
    slopes = _alibi_slopes(A_Q_HEADS)

    @pl.when(first_step)
    def _():
        dist = BAND + qi - ki
        in_band = (dist >= 0) & (dist <= A_WINDOW - 1)
        distf = dist.astype(F32)
        for h in range(A_Q_HEADS):
            bias_ref[h] = jnp.where(in_band, -slopes[h] * distf, MASK_VALUE)

    fix = jnp.where((i > 0) | (ki >= BAND), 0.0, MASK_VALUE)
    q = q_ref[0]
    k2 = jnp.concatenate([kp_ref[0], kc_ref[0]], axis=0).astype(BF16)
    v2t = jnp.concatenate([vp_ref[0], vc_ref[0]], axis=0).T
    row_v = lax.broadcasted_iota(jnp.int32, (HEAD_DIM, 2 * BAND), 0)
    ones_row = jnp.where(row_v == 0, 1.0, 0.0)
    ss, sinks = [], []
    for hk in range(A_KV_HEADS):
        heads = range(hk * A_GROUP, (hk + 1) * A_GROUP)
        kv = slice(hk * HEAD_DIM, (hk + 1) * HEAD_DIM)
        q_grp = jnp.concatenate([q[:, h * HEAD_DIM:(h + 1) * HEAD_DIM] for h in heads], axis=0)
        q_grp = (q_grp * (HEAD_DIM ** -0.5)).astype(BF16)
        s = _dot_nt(k2[:, kv], q_grp)
        ss.append(s + jnp.concatenate([bias_ref[h] + fix for h in heads], axis=1))
        sinks.append(jnp.concatenate([jnp.full((1, BAND), sink_ref[h], F32) for h in heads], axis=1))
    ms = [jnp.maximum(jnp.max(s, axis=0, keepdims=True), sk) for s, sk in zip(ss, sinks)]
    es = [jnp.exp(s - m).astype(BF16) for s, m in zip(ss, ms)]
    for hk in range(A_KV_HEADS):
        kv = slice(hk * HEAD_DIM, (hk + 1) * HEAD_DIM)
        vaug = jnp.concatenate([v2t[kv, :], ones_row], axis=0).astype(BF16)
        o = jnp.dot(vaug, es[hk], preferred_element_type=F32)
        den = o[HEAD_DIM:HEAD_DIM + 1, :] + jnp.exp(sinks[hk] - ms[hk])
        o = o[:HEAD_DIM, :] / den
        for g in range(0, A_GROUP, 2):
            pair = jnp.concatenate([o[:, g * BAND:(g + 1) * BAND], o[:, (g + 1) * BAND:(g + 2) * BAND]], axis=0)
            col = (hk * A_GROUP + g) * HEAD_DIM
            o_ref[0, :, col:col + 2 * HEAD_DIM] = pair.T.astype(o_ref.dtype)


def _swa_attention(proj, sinks):
    b, s, _ = proj.shape
    nblk = s // BAND
    prev = lambda i: jnp.maximum(i - 1, 0)
    return pl.pallas_call(
        _swa_kernel,
        out_shape=jax.ShapeDtypeStruct((b, s, A_Q_W), BF16),
        grid=(b, nblk),
        in_specs=[
            pl.BlockSpec(memory_space=pltpu.SMEM),
            pl.BlockSpec((1, BAND, A_Q_W), lambda bi, i: (bi, i, 0)),
            pl.BlockSpec((1, BAND, LANES), lambda bi, i: (bi, prev(i), COL_AK)),
            pl.BlockSpec((1, BAND, LANES), lambda bi, i: (bi, i, COL_AK)),
            pl.BlockSpec((1, BAND, LANES), lambda bi, i: (bi, prev(i), COL_AV)),
            pl.BlockSpec((1, BAND, LANES), lambda bi, i: (bi, i, COL_AV)),
        ],
        out_specs=pl.BlockSpec((1, BAND, A_Q_W), lambda bi, i: (bi, i, 0)),
        scratch_shapes=[pltpu.VMEM((A_Q_HEADS, 2 * BAND, BAND), F32)],
        compiler_params=_params(("arbitrary", "arbitrary")),
        name="swa_attention",
    )(sinks, proj, proj, proj, proj, proj)


PEN_COLS = 16


def _moba_kernel(slope_ref, q_ref, k_ref, v_ref, o_ref, kmean_ref, kaug_ref, vaug_ref, *, nb):
    hp = pl.program_id(1)
    j = pl.program_id(2)
    blk = MOBA_BLOCK
    nh = LANES // HEAD_DIM

    @pl.when(j == 0)
    def _():
        kmean_ref[...] = jnp.zeros_like(kmean_ref)
        lane_k = lax.broadcasted_iota(jnp.int32, (blk, LANES), 1)
        row_v = lax.broadcasted_iota(jnp.int32, (HEAD_DIM, blk), 0)
        ones_row = jnp.where(row_v == 0, 1.0, 0.0)
        for n in range(nb):
            kblk = k_ref[n * blk:(n + 1) * blk, :]
            kmean_ref[n:n + 1, :] = jnp.sum(kblk, axis=0, keepdims=True) * (1.0 / blk)
            onehot = jnp.where((lane_k == HEAD_DIM + n) | (lane_k == HEAD_DIM + PEN_COLS + n), 1.0, 0.0)
            vt = v_ref[n * blk:(n + 1) * blk, :].T
            for hh in range(nh):
                sl = slice(hh * HEAD_DIM, (hh + 1) * HEAD_DIM)
                k_low = kblk if hh == 0 else pltpu.roll(kblk, HEAD_DIM, 1)
                kaug_ref[hh, n] = jnp.where(lane_k < HEAD_DIM, k_low, onehot).astype(BF16)
                vaug_ref[hh, n] = jnp.concatenate([vt[sl, :], ones_row], axis=0).astype(BF16)

    ki = lax.broadcasted_iota(jnp.int32, (blk, blk), 0)
    qi = lax.broadcasted_iota(jnp.int32, (blk, blk), 1)
    rel = (qi - ki).astype(F32)
    causal = qi >= ki
    blk_id = lax.broadcasted_iota(jnp.int32, (PEN_COLS, blk), 0)
    qt_both = q_ref[...].T
    head_sl = [slice(hh * HEAD_DIM, (hh + 1) * HEAD_DIM) for hh in range(nh)]
    gates = [lax.dot_general(kmean_ref[:PEN_COLS, sl], q_ref[:, sl], (((1,), (1,)), ((), ())),
                             precision=lax.Precision.HIGHEST, preferred_element_type=F32) for sl in head_sl]
    qt_aug, qt_own, bias = [], [], []
    for hh in range(nh):
        sl = head_sl[hh]
        slope = slope_ref[hp * nh + hh]
        g = jnp.where(blk_id < j, gates[hh], NEG_INF)
        sel = jnp.zeros((PEN_COLS, blk), jnp.bool_)
        for _ in range(MOBA_TOPK):
            m = jnp.max(g, axis=0, keepdims=True)
            idx = jnp.min(jnp.where(g == m, blk_id, PEN_COLS), axis=0, keepdims=True)
            hit = blk_id == idx
            sel = sel | (hit & (m > NEG_INF))
            g = jnp.where(hit, NEG_INF, g)
        block_off = -slope * ((j - blk_id) * blk).astype(F32)
        pen = jnp.where(sel, block_off, MASK_VALUE)
        pen_hi = pen.astype(BF16)
        pen_lo = (pen - pen_hi.astype(F32)).astype(BF16)
        qt = (qt_both[sl, :] * (HEAD_DIM ** -0.5)).astype(BF16)
        qt_aug.append(jnp.concatenate([qt, pen_hi, pen_lo, jnp.zeros((HEAD_DIM - 2 * PEN_COLS, blk), BF16)], axis=0))
        qt_own.append(jnp.concatenate([qt, jnp.zeros((HEAD_DIM, blk), BF16)], axis=0))
        bias.append(-slope * rel)

    def scores(n, qs):
        return tuple(jnp.dot(kaug_ref[hh, n], qs[hh], preferred_element_type=F32) + bias[hh] for hh in range(nh))

    def pair_blocks(t):
        return jnp.minimum(2 * t, nb - 1), jnp.minimum(2 * t + 1, nb - 1)

    def pair_scores(t):
        return tuple(scores(n, qt_aug) for n in pair_blocks(t))

    s_own = [jnp.where(causal, s, MASK_VALUE) for s in scores(j, qt_own)]
    first = pair_scores(0)
    m0 = [jnp.max(s, axis=0, keepdims=True) for s in s_own]
    p0 = [jnp.exp(s - m).astype(BF16) for s, m in zip(s_own, m0)]
    o0 = [jnp.dot(vaug_ref[hh, j], p0[hh], preferred_element_type=F32) for hh in range(nh)]

    def body(t, carry):
        ms, os_, s_cur = carry
        blocks = pair_blocks(t)
        s_next = pair_scores(t + 1)
        new_m, new_o = [], []
        for hh in range(nh):
            m_new = ms[hh]
            for sb in s_cur:
                m_new = jnp.maximum(m_new, jnp.max(sb[hh], axis=0, keepdims=True))
            acc = jnp.exp(ms[hh] - m_new) * os_[hh]
            for n, sb in zip(blocks, s_cur):
                acc += jnp.dot(vaug_ref[hh, n], jnp.exp(sb[hh] - m_new).astype(BF16), preferred_element_type=F32)
            new_m.append(m_new)
            new_o.append(acc)
        return tuple(new_m), tuple(new_o), s_next

    _, o_fin, _ = lax.fori_loop(0, (j + 1) // 2, body, (tuple(m0), tuple(o0), first))
    outs = [o[:HEAD_DIM, :] / o[HEAD_DIM:HEAD_DIM + 1, :] for o in o_fin]
    o_ref[...] = jnp.concatenate(outs, axis=0).T.astype(o_ref.dtype)


def _moba_attention(proj):
    b, s, _ = proj.shape
    nb = s // MOBA_BLOCK
    assert nb <= PEN_COLS
    npair = B_W // LANES
    nh = LANES // HEAD_DIM
    slopes = jnp.asarray(_alibi_slopes(B_HEADS), F32)
    return pl.pallas_call(
        functools.partial(_moba_kernel, nb=nb),
        out_shape=jax.ShapeDtypeStruct((b, s, B_W), BF16),
        grid=(b, npair, nb),
        in_specs=[
            pl.BlockSpec(memory_space=pltpu.SMEM),
            pl.BlockSpec((None, MOBA_BLOCK, LANES), lambda bi, hp, j: (bi, j, COL_BQ + hp)),
            pl.BlockSpec((None, s, LANES), lambda bi, hp, j: (bi, 0, COL_BK + hp)),
            pl.BlockSpec((None, s, LANES), lambda bi, hp, j: (bi, 0, COL_BV + hp)),
        ],
        out_specs=pl.BlockSpec((None, MOBA_BLOCK, LANES), lambda bi, hp, j: (bi, j, hp)),
        scratch_shapes=[pltpu.VMEM((LANES, LANES), F32),
                        pltpu.VMEM((nh, nb, MOBA_BLOCK, LANES), BF16),
                        pltpu.VMEM((nh, nb, LANES, MOBA_BLOCK), BF16)],
        compiler_params=_params(("arbitrary", "arbitrary", "arbitrary")),
        name="moba_attention",
    )(slopes, proj, proj, proj)


DIL_GROUP = 4


def _dilated_kernel(slope_ref, q_ref, k_ref, v_ref, o_ref, o0, o1, o2, e0, e1, e2, *, seq):
    hp = pl.program_id(1)
    o_scr = (o0, o1, o2)
    lse_scr = (e0, e1, e2)
    nh = LANES // HEAD_DIM
    ki = lax.broadcasted_iota(jnp.int32, (2 * BAND, BAND), 0)
    qi = lax.broadcasted_iota(jnp.int32, (2 * BAND, BAND), 1)
    dist = BAND + qi - ki
    distf = dist.astype(F32)
    first_fix = jnp.where(ki >= BAND, 0.0, MASK_VALUE)
    row_v = lax.broadcasted_iota(jnp.int32, (HEAD_DIM, 2 * BAND), 0)
    ones_row = jnp.where(row_v == 0, 1.0, 0.0)

    for p, (window, r) in enumerate(DILATED_PAIRS):
        nblk = seq // (r * BAND)
        in_band = (dist >= 0) & (dist <= window // r)
        biases = [jnp.where(in_band, -(slope_ref[hp * nh + hh] * float(r)) * distf, MASK_VALUE) for hh in range(nh)]

        def group(t, carry, p=p, r=r, biases=biases):
            rows, vts, ss = [], [], []
            for gidx in range(DIL_GROUP):
                tb = t * DIL_GROUP + gidx
                c = tb % r
                i = tb // r
                cur = c + i * (BAND * r)
                prv = c + jnp.maximum(i - 1, 0) * (BAND * r)
                rows_cur = pl.ds(cur, BAND, stride=r) if r > 1 else pl.ds(cur, BAND)
                rows_prv = pl.ds(prv, BAND, stride=r) if r > 1 else pl.ds(prv, BAND)
                q = (q_ref[rows_cur, :] * (HEAD_DIM ** -0.5)).astype(BF16)
                k2 = jnp.concatenate([k_ref[rows_prv, :], k_ref[rows_cur, :]], axis=0).astype(BF16)
                fix = jnp.where(i > 0, 0.0, first_fix)
                for hh in range(nh):
                    sl = slice(hh * HEAD_DIM, (hh + 1) * HEAD_DIM)
                    ss.append(_dot_nt(k2[:, sl], q[:, sl]) + biases[hh] + fix)
                rows.append(rows_cur)
                vts.append(jnp.concatenate([v_ref[rows_prv, :], v_ref[rows_cur, :]], axis=0).T)
            ms = [jnp.max(sc, axis=0, keepdims=True) for sc in ss]
            es = [jnp.exp(sc - m).astype(BF16) for sc, m in zip(ss, ms)]
            os_ = []
            for gidx in range(DIL_GROUP):
                for hh in range(nh):
                    sl = slice(hh * HEAD_DIM, (hh + 1) * HEAD_DIM)
                    vaug = jnp.concatenate([vts[gidx][sl, :], ones_row], axis=0).astype(BF16)
                    os_.append(jnp.dot(vaug, es[gidx * nh + hh], preferred_element_type=F32))
            for gidx in range(DIL_GROUP):
                outs, lses = [], []
                for hh in range(nh):
                    o = os_[gidx * nh + hh]
                    l = o[HEAD_DIM:HEAD_DIM + 1, :]
                    outs.append(o[:HEAD_DIM, :] / l)
                    lses.append(jnp.broadcast_to(ms[gidx * nh + hh] + jnp.log(l), (HEAD_DIM, BAND)))
                o_scr[p][rows[gidx], :] = jnp.concatenate(outs, axis=0).T
                lse_scr[p][rows[gidx], :] = jnp.concatenate(lses, axis=0).T
            return carry

        lax.fori_loop(0, r * nblk // DIL_GROUP, group, 0)

    chunk = 512

    def combine(t, carry):
        rows = pl.ds(pl.multiple_of(t * chunk, chunk), chunk)
        ls = [e[rows, :] for e in lse_scr]
        m = jnp.maximum(jnp.maximum(ls[0], ls[1]), ls[2])
        ws = [jnp.exp(l - m) for l in ls]
        den = ws[0] + ws[1] + ws[2]
        num = ws[0] * o_scr[0][rows, :] + ws[1] * o_scr[1][rows, :] + ws[2] * o_scr[2][rows, :]
        o_ref[rows, :] = (num / den).astype(o_ref.dtype)
        return carry

    lax.fori_loop(0, seq // chunk, combine, 0)


def _dilated_attention(proj):
    b, s, _ = proj.shape
    npair = C_W // LANES
    slopes = jnp.asarray(_alibi_slopes(C_HEADS), F32)
    spec = lambda col: pl.BlockSpec((None, s, LANES), lambda bi, hp: (bi, 0, col + hp))
    return pl.pallas_call(
        functools.partial(_dilated_kernel, seq=s),
        out_shape=jax.ShapeDtypeStruct((b, s, C_W), BF16),
        grid=(b, npair),
        in_specs=[pl.BlockSpec(memory_space=pltpu.SMEM), spec(COL_CQ), spec(COL_CK), spec(COL_CV)],
        out_specs=pl.BlockSpec((None, s, LANES), lambda bi, hp: (bi, 0, hp)),
        scratch_shapes=[pltpu.VMEM((s, LANES), F32) for _ in range(6)],
        compiler_params=_params(("arbitrary", "arbitrary")),
        name="dilated_attention",
    )(slopes, proj, proj, proj)


def _outproj_ln_kernel(oa_ref, ob_ref, oc_ref, w_ref, x_ref, g_ref, b_ref, o_ref):
    acc = jnp.dot(oa_ref[...], w_ref[0:A_Q_W, :], preferred_element_type=F32)
    acc += jnp.dot(ob_ref[...], w_ref[A_Q_W:A_Q_W + B_W, :], preferred_element_type=F32)
    acc += jnp.dot(oc_ref[...], w_ref[A_Q_W + B_W:, :], preferred_element_type=F32)
    o_ref[...] = _layer_norm_rows(DEEPNORM_ALPHA * x_ref[...] + acc, g_ref[...], b_ref[...])


def _outproj_ln(oa, ob, oc, w_bf, x2d, g, b):
    t, d = x2d.shape
    tm = 512
    row = lambda w: pl.BlockSpec((tm, w), lambda i: (i, 0))
    full = lambda shp: pl.BlockSpec(shp, lambda i: (0, 0))
    return pl.pallas_call(
        _outproj_ln_kernel,
        out_shape=jax.ShapeDtypeStruct((t, d), F32),
        grid=(t // tm,),
        in_specs=[row(A_Q_W), row(B_W), row(C_W), full(w_bf.shape), row(d), full((1, d)), full((1, d))],
        out_specs=row(d),
        compiler_params=_params(("arbitrary",)),
        name="outproj_ln",
    )(oa, ob, oc, w_bf, x2d, g.reshape(1, d), b.reshape(1, d))


def _memkv_kernel(m_ref, w_ref, o_ref):
    o_ref[...] = jnp.dot(m_ref[...].astype(BF16), w_ref[...], preferred_element_type=F32).astype(o_ref.dtype)


def _mem_kv(mem2d, wkv_bf):
    t, d = mem2d.shape
    n = wkv_bf.shape[1]
    tm = min(512, t)
    return pl.pallas_call(
        _memkv_kernel,
        out_shape=jax.ShapeDtypeStruct((t, n), BF16),
        grid=(t // tm,),
        in_specs=[pl.BlockSpec((tm, d), lambda i: (i, 0)), pl.BlockSpec((d, n), lambda i: (0, 0))],
        out_specs=pl.BlockSpec((tm, n), lambda i: (i, 0)),
        compiler_params=_params(("arbitrary",)),
        name="mem_kv",
    )(mem2d, wkv_bf)


def _split_bf16(a):
    hi = a.astype(BF16)
    lo = (a - hi.astype(F32)).astype(BF16)
    return hi, lo


def _cross_ln_router_kernel(x_ref, kv_ref, wq_ref, wo_ref, g_ref, b_ref, wr_ref, br_ref,
                            x2_ref, slab_ref, idx_ref, gate_ref, *, n_experts):
    x = x_ref[...]
    tq = x.shape[0]
    mw = MEM_HEADS * MEM_HEAD_DIM
    q = jnp.dot(x.astype(BF16), wq_ref[...], preferred_element_type=F32).astype(BF16)
    heads = []
    for h in range(MEM_HEADS):
        sl = slice(h * MEM_HEAD_DIM, (h + 1) * MEM_HEAD_DIM)
        s = _dot_nt(q[:, sl], kv_ref[:, sl]) * (MEM_HEAD_DIM ** -0.5)
        m = jnp.max(s, axis=-1, keepdims=True)
        e = jnp.exp(s - m)
        l = jnp.sum(e, axis=-1, keepdims=True)
        o = jnp.dot(e.astype(BF16), kv_ref[:, mw + h * MEM_HEAD_DIM:mw + (h + 1) * MEM_HEAD_DIM],
                    preferred_element_type=F32) / l
        heads.append(o.astype(BF16))
    o = jnp.concatenate(heads, axis=1)
    y = jnp.dot(o, wo_ref[...], preferred_element_type=F32)
    x2 = _layer_norm_rows(DEEPNORM_ALPHA * x + y, g_ref[...], b_ref[...])
    x2_ref[...] = x2
    for c in range(SLAB):
        slab_ref[pl.ds(c, tq, stride=SLAB), :] = x2[:, c * LANES:(c + 1) * LANES]

    xh, xl = _split_bf16(x2)
    hi_both = jnp.dot(xh, wr_ref[...], preferred_element_type=F32)
    logits = (hi_both[:, :LANES] + hi_both[:, LANES:]
              + jnp.dot(xl, wr_ref[:, :LANES], preferred_element_type=F32)) + br_ref[...]
    lane = lax.broadcasted_iota(jnp.int32, (tq, LANES), 1)
    g = jnp.where(lane < n_experts, logits, NEG_INF)
    idx_out = jnp.zeros((tq, LANES), jnp.int32)
    val_out = jnp.zeros((tq, LANES), F32)
    top = None
    den = jnp.zeros((tq, 1), F32)
    for r in range(TOPK):
        m = jnp.max(g, axis=-1, keepdims=True)
        idx = jnp.min(jnp.where(g == m, lane, LANES), axis=-1, keepdims=True)
        g = jnp.where(lane == idx, NEG_INF, g)
        top = m if top is None else top
        e = jnp.exp(m - top)
        den = den + e
        idx_out = jnp.where(lane == r, idx, idx_out)
        val_out = jnp.where(lane == r, e, val_out)
    idx_ref[...] = idx_out
    gate_ref[...] = val_out / den


def _cross_ln_router(x3d, kv, wq_bf, wo_bf, g, b, w_router, b_router):
    bsz, s, d = x3d.shape
    m = kv.shape[1]
    n_experts = w_router.shape[1]
    tq = 512
    nq = s // tq
    wr = jnp.concatenate(_split_bf16(jnp.pad(w_router, ((0, 0), (0, LANES - n_experts)))), axis=1)
    br = jnp.pad(b_router, (0, LANES - n_experts)).reshape(1, LANES)
    full = lambda shp: pl.BlockSpec(shp, lambda bi, i: (0,) * len(shp))
    t = bsz * s
    return pl.pallas_call(
        functools.partial(_cross_ln_router_kernel, n_experts=n_experts),
        out_shape=(jax.ShapeDtypeStruct((t, d), F32), jax.ShapeDtypeStruct((t * SLAB, LANES), F32),
                   jax.ShapeDtypeStruct((t, LANES), jnp.int32), jax.ShapeDtypeStruct((t, LANES), F32)),
        grid=(bsz, nq),
        in_specs=[
            pl.BlockSpec((None, tq, d), lambda bi, i: (bi, i, 0)),
            pl.BlockSpec((None, m, kv.shape[2]), lambda bi, i: (bi, 0, 0)),
            full(wq_bf.shape), full(wo_bf.shape), full((1, d)), full((1, d)), full(wr.shape), full((1, LANES)),
        ],
        out_specs=(
            pl.BlockSpec((tq, d), lambda bi, i: (bi * nq + i, 0)),
            pl.BlockSpec((tq * SLAB, LANES), lambda bi, i: (bi * nq + i, 0)),
            pl.BlockSpec((tq, LANES), lambda bi, i: (bi * nq + i, 0)),
            pl.BlockSpec((tq, LANES), lambda bi, i: (bi * nq + i, 0)),
        ),
        compiler_params=_params(("arbitrary", "arbitrary")),
        name="cross_ln_router",
    )(x3d, kv, wq_bf, wo_bf, g.reshape(1, d), b.reshape(1, d), wr, br)


def _row_copy(src_hbm, src_row, dst, dst_row, sem):
    return pltpu.make_async_copy(src_hbm.at[pl.ds(src_row * SLAB, SLAB), :],
                                 dst.at[pl.ds(dst_row * PITCH, SLAB), :], sem)


ISSUE_UNROLL = 8


def _issue_rows(idx_ref, n, src_hbm, dst, sem):
    def body(g, carry):
        r0 = g * ISSUE_UNROLL
        for u in range(ISSUE_UNROLL):
            _row_copy(src_hbm, idx_ref[0, r0 + u], dst, r0 + u, sem).start(priority=u % 2)
        return carry

    lax.fori_loop(0, n // ISSUE_UNROLL, body, 0)


def _wait_rows(n, src_hbm, dst, sem):
    pltpu.make_async_copy(src_hbm.at[pl.ds(0, n * SLAB), :], dst.at[pl.ds(0, n * SLAB), :], sem).wait()


ROW_CHUNK = 128


def _row_chunk(rows):
    return min(ROW_CHUNK, rows)


def _gather_ring(n_tiles, idx_refs, rows, src_hbm, slabs, sem, consume, chunks_ref=None):
    depth = len(slabs)
    i = pl.program_id(0)

    def start(tile, idx_ref, slot):
        n = rows if chunks_ref is None else chunks_ref[tile] * _row_chunk(rows)
        _issue_rows(idx_ref, n, src_hbm, slabs[slot], sem.at[slot])

    def finish(tile, slot):
        if chunks_ref is None:
            _wait_rows(rows, src_hbm, slabs[slot], sem.at[slot])
        else:
            for q in range(rows // _row_chunk(rows)):
                @pl.when(q < chunks_ref[tile])
                def _():
                    _wait_rows(_row_chunk(rows), src_hbm, slabs[slot], sem.at[slot])

    @pl.when(i == 0)
    def _():
        for d in range(min(depth - 1, n_tiles)):
            start(d, idx_refs[d], d)

    for s in range(depth):
        @pl.when(i % depth == s)
        def _(s=s):
            @pl.when(i + depth - 1 < n_tiles)
            def _():
                start(i + depth - 1, idx_refs[depth - 1], (s + depth - 1) % depth)

            finish(i, s)
            consume(slabs[s])


def _ring_index_specs(n_tiles, width, depth):
    return [pl.BlockSpec((None, 1, width), lambda i, d=d: (jnp.minimum(i + d, n_tiles - 1), 0, 0),
                         memory_space=pltpu.SMEM) for d in range(depth)]


DISPATCH_DEPTH = 3


def _gather_kernel(*refs, tm, n_tiles):
    chunks_ref = refs[0]
    tok_refs = refs[1:DISPATCH_DEPTH + 1]
    x_hbm, o_ref = refs[DISPATCH_DEPTH + 1:DISPATCH_DEPTH + 3]
    slabs = refs[DISPATCH_DEPTH + 3:2 * DISPATCH_DEPTH + 3]
    sem = refs[2 * DISPATCH_DEPTH + 3]

    @pl.when(pl.program_id(0) == 0)
    def _():
        for slab in slabs:
            slab[...] = jnp.zeros_like(slab)

    def relayout(slab):
        for c in range(SLAB):
            o_ref[:, c * LANES:(c + 1) * LANES] = slab[pl.ds(c, tm, stride=PITCH), :].astype(o_ref.dtype)

    _gather_ring(n_tiles, tok_refs, tm, x_hbm, slabs, sem, relayout, chunks_ref)


def _dispatch_gather(row_tok, tile_chunks, x_slab, n_rows, tm, d):
    n_tiles = n_rows // tm
    tok3 = row_tok.reshape(n_tiles, 1, tm)
    return pl.pallas_call(
        functools.partial(_gather_kernel, tm=tm, n_tiles=n_tiles),
        out_shape=jax.ShapeDtypeStruct((n_rows, d), BF16),
        grid=(n_tiles,),
        in_specs=[pl.BlockSpec(memory_space=pltpu.SMEM)] + _ring_index_specs(n_tiles, tm, DISPATCH_DEPTH)
        + [pl.BlockSpec(memory_space=pl.ANY)],
        out_specs=pl.BlockSpec((tm, d), lambda i: (i, 0)),
        scratch_shapes=[pltpu.VMEM((tm * PITCH, LANES), F32) for _ in range(DISPATCH_DEPTH)]
        + [pltpu.SemaphoreType.DMA((DISPATCH_DEPTH,))],
        compiler_params=_params(("arbitrary",)),
        name="moe_dispatch",
    )(tile_chunks, *([tok3] * DISPATCH_DEPTH), x_slab)


def _expert_kernel(te_ref, tv_ref, x_ref, wg_ref, wu_ref, bg_ref, bu_ref, wd_ref, bd_ref, o_ref, acc_ref, *, tm):
    i = pl.program_id(0)
    f = pl.program_id(1)
    nf = pl.num_programs(1)

    @pl.when((tv_ref[i] == 0) & (f == nf - 1))
    def _():
        o_ref[...] = jnp.zeros_like(o_ref)

    @pl.when((i == 0) & (f == 0))
    def _():
        acc_ref[...] = jnp.zeros_like(acc_ref)

    @pl.when(tv_ref[i] > 0)
    def _():
        x = x_ref[...]
        gate = jnp.dot(x, wg_ref[...], preferred_element_type=F32) + bg_ref[...]
        up = jnp.dot(x, wu_ref[...], preferred_element_type=F32) + bu_ref[...]
        gate = jnp.minimum(gate, SWIGLU_LIMIT)
        up = jnp.clip(up, -SWIGLU_LIMIT, SWIGLU_LIMIT)
        act = (up + 1.0) * (gate * jax.nn.sigmoid(SWIGLU_ALPHA * gate))
        prev = jnp.where(f == 0, jnp.broadcast_to(bd_ref[...], acc_ref.shape), acc_ref[...])
        val = prev + jnp.dot(act.astype(BF16), wd_ref[...], preferred_element_type=F32)
        acc_ref[...] = val
        for c in range(SLAB):
            o_ref[pl.ds(c, tm, stride=SLAB), :] = val[:, c * LANES:(c + 1) * LANES]


def _expert_mlp(tile_expert, tile_valid, xs, wgu_bf, b_gu, wd_bf, b_down, layer, tm, tf):
    n_rows, d = xs.shape
    _, n_exp, _, two_ff = wgu_bf.shape
    d_ff = two_ff // 2
    nf = d_ff // tf
    wf = lambda i, f, tv: jnp.where(tv[i] > 0, f, nf - 1)
    return pl.pallas_call(
        functools.partial(_expert_kernel, tm=tm),
        out_shape=jax.ShapeDtypeStruct((n_rows * SLAB, LANES), F32),
        grid_spec=pltpu.PrefetchScalarGridSpec(
            num_scalar_prefetch=2,
            grid=(n_rows // tm, nf),
            in_specs=[
                pl.BlockSpec((tm, d), lambda i, f, te, tv: (i, 0)),
                pl.BlockSpec((None, None, d, tf), lambda i, f, te, tv: (layer, te[i], 0, wf(i, f, tv))),
                pl.BlockSpec((None, None, d, tf), lambda i, f, te, tv: (layer, te[i], 0, nf + wf(i, f, tv))),
                pl.BlockSpec((None, 1, tf), lambda i, f, te, tv: (te[i], 0, wf(i, f, tv))),
                pl.BlockSpec((None, 1, tf), lambda i, f, te, tv: (te[i], 0, nf + wf(i, f, tv))),
                pl.BlockSpec((None, None, tf, d), lambda i, f, te, tv: (layer, te[i], wf(i, f, tv), 0)),
                pl.BlockSpec((None, 1, d), lambda i, f, te, tv: (te[i], 0, 0)),
            ],
            out_specs=pl.BlockSpec((tm * SLAB, LANES), lambda i, f, te, tv: (i, 0)),
            scratch_shapes=[pltpu.VMEM((tm, d), F32)],
        ),
        compiler_params=_params(("arbitrary", "arbitrary")),
        name="moe_experts",
    )(tile_expert, tile_valid, xs, wgu_bf, wgu_bf, b_gu.reshape(n_exp, 1, two_ff), b_gu.reshape(n_exp, 1, two_ff),
      wd_bf, b_down.reshape(n_exp, 1, d))


COMBINE_DEPTH = 2


def _combine_ln_kernel(*refs, tt, n_tiles):
    pos_refs = refs[:COMBINE_DEPTH]
    y_hbm, gate_ref, x_ref, g_ref, b_ref, o_ref = refs[COMBINE_DEPTH:COMBINE_DEPTH + 6]
    slabs = refs[COMBINE_DEPTH + 6:2 * COMBINE_DEPTH + 6]
    y_scr, sem = refs[2 * COMBINE_DEPTH + 6:]

    def weighted_sum(slab):
        gates = gate_ref[...]
        gk = [jnp.broadcast_to(gates[:, k:k + 1], (tt, LANES)) for k in range(TOPK)]
        for c in range(SLAB):
            acc = gk[0] * slab[pl.ds(c, tt, stride=PITCH), :]
            for k in range(1, TOPK):
                acc += gk[k] * slab[pl.ds(k * tt * PITCH + c, tt, stride=PITCH), :]
            y_scr[:, c * LANES:(c + 1) * LANES] = acc

    _gather_ring(n_tiles, pos_refs, tt * TOPK, y_hbm, slabs, sem, weighted_sum)
    o_ref[...] = _layer_norm_rows(DEEPNORM_ALPHA * x_ref[...] + y_scr[...], g_ref[...], b_ref[...])


def _combine_ln(pos, y_slab, gates, x2d, g, b):
    t, d = x2d.shape
    tt = 256
    n_tiles = t // tt
    pos3 = pos.reshape(n_tiles, tt, TOPK).transpose(0, 2, 1).reshape(n_tiles, 1, TOPK * tt)
    stage = pltpu.VMEM((tt * TOPK * PITCH, LANES), F32)
    return pl.pallas_call(
        functools.partial(_combine_ln_kernel, tt=tt, n_tiles=n_tiles),
        out_shape=jax.ShapeDtypeStruct((t, d), F32),
        grid=(n_tiles,),
        in_specs=_ring_index_specs(n_tiles, tt * TOPK, COMBINE_DEPTH) + [
            pl.BlockSpec(memory_space=pl.ANY),
            pl.BlockSpec((tt, LANES), lambda i: (i, 0)),
            pl.BlockSpec((tt, d), lambda i: (i, 0)),
            pl.BlockSpec((1, d), lambda i: (0, 0)),
            pl.BlockSpec((1, d), lambda i: (0, 0)),
        ],
        out_specs=pl.BlockSpec((tt, d), lambda i: (i, 0)),
        scratch_shapes=[stage] * COMBINE_DEPTH + [pltpu.VMEM((tt, d), F32),
                                                  pltpu.SemaphoreType.DMA((COMBINE_DEPTH,))],
        compiler_params=_params(("arbitrary",)),
        name="moe_combine_ln",
    )(*([pos3] * COMBINE_DEPTH), y_slab, gates, x2d, g.reshape(1, d), b.reshape(1, d))


def _routing_tables(top_idx, n_experts, tm):
    t = top_idx.shape[0]
    n_assign = t * TOPK
    e_flat = top_idx.reshape(-1)
    onehot = (e_flat[:, None] == jnp.arange(n_experts, dtype=jnp.int32)[None, :]).astype(jnp.int32)
    csum = jnp.cumsum(onehot, axis=0)
    counts = csum[-1]
    rank = jnp.sum((csum - onehot) * onehot, axis=1)
    padded = (counts + tm - 1) // tm * tm
    p_end = jnp.cumsum(padded)
    p_start = p_end - padded
    start = jnp.cumsum(counts) - counts
    pos = (p_start[e_flat] + rank).astype(jnp.int32)
    n_tiles = (n_assign + n_experts * (tm - 1)) // tm
    n_rows = n_tiles * tm
    order = jnp.argsort(e_flat)
    tile_expert = jnp.minimum(jnp.searchsorted(p_end, jnp.arange(n_tiles) * tm, side="right"),
                              n_experts - 1).astype(jnp.int32)
    tile_valid = (jnp.arange(n_tiles) * tm < p_end[-1]).astype(jnp.int32)
    rows = jnp.arange(n_rows)
    row_e = jnp.repeat(tile_expert, tm)
    off = rows - p_start[row_e]
    real = (off < counts[row_e]) & (rows < p_end[-1])
    src = jnp.clip(start[row_e] + off, 0, n_assign - 1)
    row_tok = jnp.where(real, order[src] // TOPK, 0).astype(jnp.int32)
    tile_fill = jnp.clip(jnp.sum(real.reshape(n_tiles, tm), axis=1), 0, tm)
    chunk = _row_chunk(tm)
    tile_chunks = ((tile_fill + chunk - 1) // chunk).astype(jnp.int32)
    return row_tok, pos, tile_expert, tile_valid, tile_chunks, n_rows


MOE_TM = 512
MOE_TF = 1024


def kernel(x, mem, w_in, b_in, attn_sinks, w_out, ln1_g, ln1_b, w_mem_q, w_mem_k, w_mem_v, w_mem_o, ln2_g, ln2_b,
           w_router, b_router, w_gate_up, b_gate_up, w_down, b_down, ln3_g, ln3_b):
    bsz, s, d = x.shape
    t = bsz * s
    depth = w_in.shape[0]
    n_experts = w_router.shape[-1]
    mem2d = mem.reshape(bsz * mem.shape[1], d)
    x2d = x.reshape(t, d)
    wgu_bf = w_gate_up.astype(BF16)
    wd_bf = w_down.astype(BF16)
    tf = min(MOE_TF, w_down.shape[2])
    for i in range(depth):
        proj = _inproj(x2d, w_in[i].astype(BF16), b_in[i]).reshape(bsz, s, IN_WIDTH)
        oa = _swa_attention(proj, attn_sinks[i])
        ob = _moba_attention(proj)
        oc = _dilated_attention(proj)
        x1 = _outproj_ln(oa.reshape(t, A_Q_W), ob.reshape(t, B_W), oc.reshape(t, C_W), w_out[i].astype(BF16), x2d,
                         ln1_g[i], ln1_b[i])
        wkv = jnp.concatenate([w_mem_k[i], w_mem_v[i]], axis=1).astype(BF16)
        kv = _mem_kv(mem2d, wkv).reshape(bsz, mem.shape[1], wkv.shape[1])
        x2, x2_slab, top_idx, gates = _cross_ln_router(x1.reshape(bsz, s, d), kv, w_mem_q[i].astype(BF16),
                                                       w_mem_o[i].astype(BF16), ln2_g[i], ln2_b[i],
                                                       w_router[i], b_router[i])
        row_tok, pos, tile_expert, tile_valid, tile_chunks, n_rows = _routing_tables(top_idx[:, :TOPK], n_experts,
                                                                                     MOE_TM)
        xs = _dispatch_gather(row_tok, tile_chunks, x2_slab, n_rows, MOE_TM, d)
        y_slab = _expert_mlp(tile_expert, tile_valid, xs, wgu_bf, b_gate_up[i], wd_bf, b_down[i], i, MOE_TM, tf)
        x2d = _combine_ln(pos, y_slab, gates, x2, ln3_g[i], ln3_b[i])
    return x2d.reshape(bsz, s, d)
```

```python
import functools
from typing import NamedTuple

import numpy as np
import jax
import jax.numpy as jnp
from jax import lax
from jax.experimental import pallas as pl
from jax.experimental.pallas import tpu as pltpu

F32 = jnp.float32
BF16 = jnp.bfloat16
NEG_INF = float("-inf")
MASK_VALUE = -1e30

LANES = 128
SUBLANES = 8
VMEM_LIMIT_BYTES = 56 * 1024 * 1024

HEAD_DIM = 64
A_Q_HEADS = 16
A_KV_HEADS = 2
A_GROUP = A_Q_HEADS // A_KV_HEADS
A_WINDOW = 128
B_HEADS = 8
MOBA_BLOCK = 256
MOBA_TOPK = 3
C_HEADS = 8
DILATED_PAIRS = ((128, 1), (512, 4), (2048, 16))
BAND = 128
A_Q_W = A_Q_HEADS * HEAD_DIM
A_KV_W = A_KV_HEADS * HEAD_DIM
B_W = B_HEADS * HEAD_DIM
C_W = C_HEADS * HEAD_DIM
IN_WIDTH = A_Q_W + 2 * A_KV_W + 3 * B_W + 3 * C_W
COL_AK = A_Q_W // LANES
COL_AV = COL_AK + A_KV_W // LANES
COL_BQ = COL_AV + A_KV_W // LANES
COL_BK = COL_BQ + B_W // LANES
COL_BV = COL_BK + B_W // LANES
COL_CQ = COL_BV + B_W // LANES
COL_CK = COL_CQ + C_W // LANES
COL_CV = COL_CK + C_W // LANES
MEM_HEADS = 4
MEM_HEAD_DIM = 128
TOPK = 4
SWIGLU_LIMIT = 7.0
SWIGLU_ALPHA = 1.702
LN_EPS = 1e-5
DEPTH = 2
DEEPNORM_ALPHA = (2 * DEPTH) ** 0.25
SLAB = 16
PITCH = 24
PACK = SLAB // 2


class RowLayout(NamedTuple):
    rows: int
    pitch: int


F32_ROW = RowLayout(SLAB, PITCH)
PACKED_ROW = RowLayout(PACK, PACK)


def _alibi_slopes(n):
    return [float(2.0 ** (-8.0 * (i + 1) / n)) for i in range(n)]


def _params(sem=None):
    return pltpu.CompilerParams(dimension_semantics=sem, vmem_limit_bytes=VMEM_LIMIT_BYTES)


def _layer_norm_rows(y, g, b):
    mu = jnp.mean(y, axis=-1, keepdims=True)
    d = y - mu
    var = jnp.mean(d * d, axis=-1, keepdims=True)
    return d * lax.rsqrt(var + LN_EPS) * g + b


def _dot_nt(a, b):
    return lax.dot_general(a, b, (((1,), (1,)), ((), ())), preferred_element_type=F32)


def _inproj_kernel(x_ref, w_ref, b_ref, o_ref):
    acc = jnp.dot(x_ref[...].astype(BF16), w_ref[...], preferred_element_type=F32)
    o_ref[...] = acc + b_ref[...]


def _inproj(x2d, w_bf, b):
    t, d = x2d.shape
    n = w_bf.shape[1]
    tm, tn = 512, n // 2
    return pl.pallas_call(
        _inproj_kernel,
        out_shape=jax.ShapeDtypeStruct((t, n), F32),
        grid=(n // tn, t // tm),
        in_specs=[
            pl.BlockSpec((tm, d), lambda j, i: (i, 0)),
            pl.BlockSpec((d, tn), lambda j, i: (0, j)),
            pl.BlockSpec((1, tn), lambda j, i: (0, j)),
        ],
        out_specs=pl.BlockSpec((tm, tn), lambda j, i: (i, j)),
        compiler_params=_params(("arbitrary", "arbitrary")),
        name="inproj",
    )(x2d, w_bf, b.reshape(1, n))


def _swa_kernel(sink_ref, q_ref, kp_ref, kc_ref, vp_ref, vc_ref, o_ref, bias_ref):
    first_step = (pl.program_id(0) == 0) & (pl.program_id(1) == 0)
    i = pl.program_id(1)
    ki = lax.broadcasted_iota(jnp.int32, (2 * BAND, BAND), 0)
    qi = lax.broadcasted_iota(jnp.int32, (2 * BAND, BAND), 1)
    slopes = _alibi_slopes(A_Q_HEADS)

    @pl.when(first_step)
    def _():
        dist = BAND + qi - ki
        in_band = (dist >= 0) & (dist <= A_WINDOW - 1)
        distf = dist.astype(F32)
        for h in range(A_Q_HEADS):
            bias_ref[h] = jnp.where(in_band, -slopes[h] * distf, MASK_VALUE)

    fix = jnp.where((i > 0) | (ki >= BAND), 0.0, MASK_VALUE)
    q = q_ref[0]
    k2 = jnp.concatenate([kp_ref[0], kc_ref[0]], axis=0).astype(BF16)
    v2t = jnp.concatenate([vp_ref[0], vc_ref[0]], axis=0).T
    row_v = lax.broadcasted_iota(jnp.int32, (HEAD_DIM, 2 * BAND), 0)
    ones_row = jnp.where(row_v == 0, 1.0, 0.0)
    ss, sinks = [], []
    for hk in range(A_KV_HEADS):
        heads = range(hk * A_GROUP, (hk + 1) * A_GROUP)
        kv = slice(hk * HEAD_DIM, (hk + 1) * HEAD_DIM)
        q_grp = jnp.concatenate([q[:, h * HEAD_DIM:(h + 1) * HEAD_DIM] for h in heads], axis=0)
        q_grp = (q_grp * (HEAD_DIM ** -0.5)).astype(BF16)
        s = _dot_nt(k2[:, kv], q_grp)
        ss.append(s + jnp.concatenate([bias_ref[h] + fix for h in heads], axis=1))
        sinks.append(jnp.concatenate([jnp.full((1, BAND), sink_ref[h], F32) for h in heads], axis=1))
    ms = [jnp.maximum(jnp.max(s, axis=0, keepdims=True), sk) for s, sk in zip(ss, sinks)]
    es = [jnp.exp(s - m).astype(BF16) for s, m in zip(ss, ms)]
    for hk in range(A_KV_HEADS):
        kv = slice(hk * HEAD_DIM, (hk + 1) * HEAD_DIM)
        vaug = jnp.concatenate([v2t[kv, :], ones_row], axis=0).astype(BF16)
        o = jnp.dot(vaug, es[hk], preferred_element_type=F32)
        den = o[HEAD_DIM:HEAD_DIM + 1, :] + jnp.exp(sinks[hk] - ms[hk])
        o = o[:HEAD_DIM, :] / den
        for g in range(0, A_GROUP, 2):
            pair = jnp.concatenate([o[:, g * BAND:(g + 1) * BAND], o[:, (g + 1) * BAND:(g + 2) * BAND]], axis=0)
            col = (hk * A_GROUP + g) * HEAD_DIM
            o_ref[0, :, col:col + 2 * HEAD_DIM] = pair.T.astype(o_ref.dtype)


def _swa_attention(proj, sinks):
    b, s, _ = proj.shape
    nblk = s // BAND
    prev = lambda i: jnp.maximum(i - 1, 0)
    return pl.pallas_call(
        _swa_kernel,
        out_shape=jax.ShapeDtypeStruct((b, s, A_Q_W), BF16),
        grid=(b, nblk),
        in_specs=[
            pl.BlockSpec(memory_space=pltpu.SMEM),
            pl.BlockSpec((1, BAND, A_Q_W), lambda bi, i: (bi, i, 0)),
            pl.BlockSpec((1, BAND, LANES), lambda bi, i: (bi, prev(i), COL_AK)),
            pl.BlockSpec((1, BAND, LANES), lambda bi, i: (bi, i, COL_AK)),
            pl.BlockSpec((1, BAND, LANES), lambda bi, i: (bi, prev(i), COL_AV)),
            pl.BlockSpec((1, BAND, LANES), lambda bi, i: (bi, i, COL_AV)),
        ],
        out_specs=pl.BlockSpec((1, BAND, A_Q_W), lambda bi, i: (bi, i, 0)),
        scratch_shapes=[pltpu.VMEM((A_Q_HEADS, 2 * BAND, BAND), F32)],
        compiler_params=_params(("arbitrary", "arbitrary")),
        name="swa_attention",
    )(sinks, proj, proj, proj, proj, proj)


PEN_COLS = 16


def _moba_kernel(slope_ref, q_ref, k_ref, v_ref, o_ref, kmean_ref, kaug_ref, vaug_ref, *, nb):
    hp = pl.program_id(1)
    j = pl.program_id(2)
    blk = MOBA_BLOCK
    nh = LANES // HEAD_DIM

    @pl.when(j == 0)
    def _():
        kmean_ref[...] = jnp.zeros_like(kmean_ref)
        lane_k = lax.broadcasted_iota(jnp.int32, (blk, LANES), 1)
        row_v = lax.broadcasted_iota(jnp.int32, (HEAD_DIM, blk), 0)
        ones_row = jnp.where(row_v == 0, 1.0, 0.0)
        for n in range(nb):
            kblk = k_ref[n * blk:(n + 1) * blk, :]
            kmean_ref[n:n + 1, :] = jnp.sum(kblk, axis=0, keepdims=True) * (1.0 / blk)
            onehot = jnp.where((lane_k == HEAD_DIM + n) | (lane_k == HEAD_DIM + PEN_COLS + n), 1.0, 0.0)
            vt = v_ref[n * blk:(n + 1) * blk, :].T
            for hh in range(nh):
                sl = slice(hh * HEAD_DIM, (hh + 1) * HEAD_DIM)
                k_low = kblk if hh == 0 else pltpu.roll(kblk, HEAD_DIM, 1)
                kaug_ref[hh, n] = jnp.where(lane_k < HEAD_DIM, k_low, onehot).astype(BF16)
                vaug_ref[hh, n] = jnp.concatenate([vt[sl, :], ones_row], axis=0).astype(BF16)

    ki = lax.broadcasted_iota(jnp.int32, (blk, blk), 0)
    qi = lax.broadcasted_iota(jnp.int32, (blk, blk), 1)
    rel = (qi - ki).astype(F32)
    causal = qi >= ki
    blk_id = lax.broadcasted_iota(jnp.int32, (PEN_COLS, blk), 0)
    qt_both = q_ref[...].T
    head_sl = [slice(hh * HEAD_DIM, (hh + 1) * HEAD_DIM) for hh in range(nh)]
    gates = [lax.dot_general(kmean_ref[:PEN_COLS, sl], q_ref[:, sl], (((1,), (1,)), ((), ())),
                             precision=lax.Precision.HIGHEST, preferred_element_type=F32) for sl in head_sl]
    qt_aug, qt_own, bias = [], [], []
    for hh in range(nh):
        sl = head_sl[hh]
        slope = slope_ref[hp * nh + hh]
        g = jnp.where(blk_id < j, gates[hh], NEG_INF)
        sel = jnp.zeros((PEN_COLS, blk), jnp.bool_)
        for _ in range(MOBA_TOPK):
            m = jnp.max(g, axis=0, keepdims=True)
            idx = jnp.min(jnp.where(g == m, blk_id, PEN_COLS), axis=0, keepdims=True)
            hit = blk_id == idx
            sel = sel | (hit & (m > NEG_INF))
            g = jnp.where(hit, NEG_INF, g)
        block_off = -slope * ((j - blk_id) * blk).astype(F32)
        pen = jnp.where(sel, block_off, MASK_VALUE)
        pen_hi = pen.astype(BF16)
        pen_lo = (pen - pen_hi.astype(F32)).astype(BF16)
        qt = (qt_both[sl, :] * (HEAD_DIM ** -0.5)).astype(BF16)
        qt_aug.append(jnp.concatenate([qt, pen_hi, pen_lo, jnp.zeros((HEAD_DIM - 2 * PEN_COLS, blk), BF16)], axis=0))
        qt_own.append(jnp.concatenate([qt, jnp.zeros((HEAD_DIM, blk), BF16)], axis=0))
        bias.append(-slope * rel)

    def scores(n, qs):
        return tuple(jnp.dot(kaug_ref[hh, n], qs[hh], preferred_element_type=F32) + bias[hh] for hh in range(nh))

    def pair_blocks(t):
        return jnp.minimum(2 * t, nb - 1), jnp.minimum(2 * t + 1, nb - 1)

    def pair_scores(t):
        return tuple(scores(n, qt_aug) for n in pair_blocks(t))

    s_own = [jnp.where(causal, s, MASK_VALUE) for s in scores(j, qt_own)]
    first = pair_scores(0)
    m0 = [jnp.max(s, axis=0, keepdims=True) for s in s_own]
    p0 = [jnp.exp(s - m).astype(BF16) for s, m in zip(s_own, m0)]
    o0 = [jnp.dot(vaug_ref[hh, j], p0[hh], preferred_element_type=F32) for hh in range(nh)]

    def body(t, carry):
        ms, os_, s_cur = carry
        blocks = pair_blocks(t)
        s_next = pair_scores(t + 1)
        new_m, new_o = [], []
        for hh in range(nh):
            m_new = ms[hh]
            for sb in s_cur:
                m_new = jnp.maximum(m_new, jnp.max(sb[hh], axis=0, keepdims=True))
            acc = jnp.exp(ms[hh] - m_new) * os_[hh]
            for n, sb in zip(blocks, s_cur):
                acc += jnp.dot(vaug_ref[hh, n], jnp.exp(sb[hh] - m_new).astype(BF16), preferred_element_type=F32)
            new_m.append(m_new)
            new_o.append(acc)
        return tuple(new_m), tuple(new_o), s_next

    _, o_fin, _ = lax.fori_loop(0, (j + 1) // 2, body, (tuple(m0), tuple(o0), first))
    outs = [o[:HEAD_DIM, :] / o[HEAD_DIM:HEAD_DIM + 1, :] for o in o_fin]
    o_ref[...] = jnp.concatenate(outs, axis=0).T.astype(o_ref.dtype)


def _moba_attention(proj):
    b, s, _ = proj.shape
    nb = s // MOBA_BLOCK
    assert nb <= PEN_COLS
    npair = B_W // LANES
    nh = LANES // HEAD_DIM
    slopes = jnp.asarray(_alibi_slopes(B_HEADS), F32)
    return pl.pallas_call(
        functools.partial(_moba_kernel, nb=nb),
        out_shape=jax.ShapeDtypeStruct((b, s, B_W), BF16),
        grid=(b, npair, nb),
        in_specs=[
            pl.BlockSpec(memory_space=pltpu.SMEM),
            pl.BlockSpec((None, MOBA_BLOCK, LANES), lambda bi, hp, j: (bi, j, COL_BQ + hp)),
            pl.BlockSpec((None, s, LANES), lambda bi, hp, j: (bi, 0, COL_BK + hp)),
            pl.BlockSpec((None, s, LANES), lambda bi, hp, j: (bi, 0, COL_BV + hp)),
        ],
        out_specs=pl.BlockSpec((None, MOBA_BLOCK, LANES), lambda bi, hp, j: (bi, j, hp)),
        scratch_shapes=[pltpu.VMEM((LANES, LANES), F32),
                        pltpu.VMEM((nh, nb, MOBA_BLOCK, LANES), BF16),
                        pltpu.VMEM((nh, nb, LANES, MOBA_BLOCK), BF16)],
        compiler_params=_params(("arbitrary", "arbitrary", "arbitrary")),
        name="moba_attention",
    )(slopes, proj, proj, proj)


DIL_GROUP = 4


def _dilated_kernel(slope_ref, q_ref, k_ref, v_ref, o_ref, o0, o1, o2, e0, e1, e2, *, seq):
    hp = pl.program_id(1)
    o_scr = (o0, o1, o2)
    lse_scr = (e0, e1, e2)
    nh = LANES // HEAD_DIM
    ki = lax.broadcasted_iota(jnp.int32, (2 * BAND, BAND), 0)
    qi = lax.broadcasted_iota(jnp.int32, (2 * BAND, BAND), 1)
    dist = BAND + qi - ki
    distf = dist.astype(F32)
    first_fix = jnp.where(ki >= BAND, 0.0, MASK_VALUE)
    row_v = lax.broadcasted_iota(jnp.int32, (HEAD_DIM, 2 * BAND), 0)
    ones_row = jnp.where(row_v == 0, 1.0, 0.0)

    for p, (window, r) in enumerate(DILATED_PAIRS):
        nblk = seq // (r * BAND)
        in_band = (dist >= 0) & (dist <= window // r)
        biases = [jnp.where(in_band, -(slope_ref[hp * nh + hh] * float(r)) * distf, MASK_VALUE) for hh in range(nh)]

        def group(t, carry, p=p, r=r, biases=biases):
            rows, vts, ss = [], [], []
            for gidx in range(DIL_GROUP):
                tb = t * DIL_GROUP + gidx
                c = tb % r
                i = tb // r
                cur = c + i * (BAND * r)
                prv = c + jnp.maximum(i - 1, 0) * (BAND * r)
                rows_cur = pl.ds(cur, BAND, stride=r) if r > 1 else pl.ds(cur, BAND)
                rows_prv = pl.ds(prv, BAND, stride=r) if r > 1 else pl.ds(prv, BAND)
                q = (q_ref[rows_cur, :] * (HEAD_DIM ** -0.5)).astype(BF16)
                k2 = jnp.concatenate([k_ref[rows_prv, :], k_ref[rows_cur, :]], axis=0).astype(BF16)
                fix = jnp.where(i > 0, 0.0, first_fix)
                for hh in range(nh):
                    sl = slice(hh * HEAD_DIM, (hh + 1) * HEAD_DIM)
                    ss.append(_dot_nt(k2[:, sl], q[:, sl]) + biases[hh] + fix)
                rows.append(rows_cur)
                vts.append(jnp.concatenate([v_ref[rows_prv, :], v_ref[rows_cur, :]], axis=0).T)
            ms = [jnp.max(sc, axis=0, keepdims=True) for sc in ss]
            es = [jnp.exp(sc - m).astype(BF16) for sc, m in zip(ss, ms)]
            os_ = []
            for gidx in range(DIL_GROUP):
                for hh in range(nh):
                    sl = slice(hh * HEAD_DIM, (hh + 1) * HEAD_DIM)
                    vaug = jnp.concatenate([vts[gidx][sl, :], ones_row], axis=0).astype(BF16)
                    os_.append(jnp.dot(vaug, es[gidx * nh + hh], preferred_element_type=F32))
            for gidx in range(DIL_GROUP):
                outs, lses = [], []
                for hh in range(nh):
                    o = os_[gidx * nh + hh]
                    l = o[HEAD_DIM:HEAD_DIM + 1, :]
                    outs.append(o[:HEAD_DIM, :] / l)
                    lses.append(jnp.broadcast_to(ms[gidx * nh + hh] + jnp.log(l), (HEAD_DIM, BAND)))
                o_scr[p][rows[gidx], :] = jnp.concatenate(outs, axis=0).T
                lse_scr[p][rows[gidx], :] = jnp.concatenate(lses, axis=0).T
            return carry

        lax.fori_loop(0, r * nblk // DIL_GROUP, group, 0)

    chunk = 512

    def combine(t, carry):
        rows = pl.ds(pl.multiple_of(t * chunk, chunk), chunk)
        ls = [e[rows, :] for e in lse_scr]
        m = jnp.maximum(jnp.maximum(ls[0], ls[1]), ls[2])
        ws = [jnp.exp(l - m) for l in ls]
        den = ws[0] + ws[1] + ws[2]
        num = ws[0] * o_scr[0][rows, :] + ws[1] * o_scr[1][rows, :] + ws[2] * o_scr[2][rows, :]
        o_ref[rows, :] = (num / den).astype(o_ref.dtype)
        return carry

    lax.fori_loop(0, seq // chunk, combine, 0)


def _dilated_attention(proj):
    b, s, _ = proj.shape
    npair = C_W // LANES
    slopes = jnp.asarray(_alibi_slopes(C_HEADS), F32)
    spec = lambda col: pl.BlockSpec((None, s, LANES), lambda bi, hp: (bi, 0, col + hp))
    return pl.pallas_call(
        functools.partial(_dilated_kernel, seq=s),
        out_shape=jax.ShapeDtypeStruct((b, s, C_W), BF16),
        grid=(b, npair),
        in_specs=[pl.BlockSpec(memory_space=pltpu.SMEM), spec(COL_CQ), spec(COL_CK), spec(COL_CV)],
        out_specs=pl.BlockSpec((None, s, LANES), lambda bi, hp: (bi, 0, hp)),
        scratch_shapes=[pltpu.VMEM((s, LANES), F32) for _ in range(6)],
        compiler_params=_params(("arbitrary", "arbitrary")),
        name="dilated_attention",
    )(slopes, proj, proj, proj)


def _outproj_ln_kernel(oa_ref, ob_ref, oc_ref, w_ref, x_ref, g_ref, b_ref, o_ref):
    acc = jnp.dot(oa_ref[...], w_ref[0:A_Q_W, :], preferred_element_type=F32)
    acc += jnp.dot(ob_ref[...], w_ref[A_Q_W:A_Q_W + B_W, :], preferred_element_type=F32)
    acc += jnp.dot(oc_ref[...], w_ref[A_Q_W + B_W:, :], preferred_element_type=F32)
    o_ref[...] = _layer_norm_rows(DEEPNORM_ALPHA * x_ref[...] + acc, g_ref[...], b_ref[...])


def _outproj_ln(oa, ob, oc, w_bf, x2d, g, b):
    t, d = x2d.shape
    tm = 512
    row = lambda w: pl.BlockSpec((tm, w), lambda i: (i, 0))
    full = lambda shp: pl.BlockSpec(shp, lambda i: (0, 0))
    return pl.pallas_call(
        _outproj_ln_kernel,
        out_shape=jax.ShapeDtypeStruct((t, d), F32),
        grid=(t // tm,),
        in_specs=[row(A_Q_W), row(B_W), row(C_W), full(w_bf.shape), row(d), full((1, d)), full((1, d))],
        out_specs=row(d),
        compiler_params=_params(("arbitrary",)),
        name="outproj_ln",
    )(oa, ob, oc, w_bf, x2d, g.reshape(1, d), b.reshape(1, d))


def _memkv_kernel(m_ref, w_ref, o_ref):
    o_ref[...] = jnp.dot(m_ref[...].astype(BF16), w_ref[...], preferred_element_type=F32).astype(o_ref.dtype)


def _mem_kv(mem2d, wkv_bf):
    t, d = mem2d.shape
    n = wkv_bf.shape[1]
    tm = min(512, t)
    return pl.pallas_call(
        _memkv_kernel,
        out_shape=jax.ShapeDtypeStruct((t, n), BF16),
        grid=(t // tm,),
        in_specs=[pl.BlockSpec((tm, d), lambda i: (i, 0)), pl.BlockSpec((d, n), lambda i: (0, 0))],
        out_specs=pl.BlockSpec((tm, n), lambda i: (i, 0)),
        compiler_params=_params(("arbitrary",)),
        name="mem_kv",
    )(mem2d, wkv_bf)


def _split_bf16(a):
    hi = a.astype(BF16)
    lo = (a - hi.astype(F32)).astype(BF16)
    return hi, lo


def _cross_ln_router_kernel(x_ref, kv_ref, wq_ref, wo_ref, g_ref, b_ref, wr_ref, br_ref,
                            x2_ref, slab_ref, idx_ref, gate_ref, *, n_experts):
    x = x_ref[...]
    tq = x.shape[0]
    mw = MEM_HEADS * MEM_HEAD_DIM
    q = jnp.dot(x.astype(BF16), wq_ref[...], preferred_element_type=F32).astype(BF16)
    heads = []
    for h in range(MEM_HEADS):
        sl = slice(h * MEM_HEAD_DIM, (h + 1) * MEM_HEAD_DIM)
        s = _dot_nt(q[:, sl], kv_ref[:, sl]) * (MEM_HEAD_DIM ** -0.5)
        m = jnp.max(s, axis=-1, keepdims=True)
        e = jnp.exp(s - m)
        l = jnp.sum(e, axis=-1, keepdims=True)
        o = jnp.dot(e.astype(BF16), kv_ref[:, mw + h * MEM_HEAD_DIM:mw + (h + 1) * MEM_HEAD_DIM],
                    preferred_element_type=F32) / l
        heads.append(o.astype(BF16))
    o = jnp.concatenate(heads, axis=1)
    y = jnp.dot(o, wo_ref[...], preferred_element_type=F32)
    x2 = _layer_norm_rows(DEEPNORM_ALPHA * x + y, g_ref[...], b_ref[...])
    x2_ref[...] = x2
    for c in range(SLAB):
        slab_ref[pl.ds(c, tq, stride=SLAB), :] = x2[:, c * LANES:(c + 1) * LANES]

    xh, xl = _split_bf16(x2)
    hi_both = jnp.dot(xh, wr_ref[...], preferred_element_type=F32)
    logits = (hi_both[:, :LANES] + hi_both[:, LANES:]
              + jnp.dot(xl, wr_ref[:, :LANES], preferred_element_type=F32)) + br_ref[...]
    lane = lax.broadcasted_iota(jnp.int32, (tq, LANES), 1)
    g = jnp.where(lane < n_experts, logits, NEG_INF)
    idx_out = jnp.zeros((tq, LANES), jnp.int32)
    val_out = jnp.zeros((tq, LANES), F32)
    top = None
    den = jnp.zeros((tq, 1), F32)
    for r in range(TOPK):
        m = jnp.max(g, axis=-1, keepdims=True)
        idx = jnp.min(jnp.where(g == m, lane, LANES), axis=-1, keepdims=True)
        g = jnp.where(lane == idx, NEG_INF, g)
        top = m if top is None else top
        e = jnp.exp(m - top)
        den = den + e
        idx_out = jnp.where(lane == r, idx, idx_out)
        val_out = jnp.where(lane == r, e, val_out)
    idx_ref[...] = idx_out
    gate_ref[...] = val_out / den


def _cross_ln_router(x3d, kv, wq_bf, wo_bf, g, b, w_router, b_router):
    bsz, s, d = x3d.shape
    m = kv.shape[1]
    n_experts = w_router.shape[1]
    tq = 512
    nq = s // tq
    wr = jnp.concatenate(_split_bf16(jnp.pad(w_router, ((0, 0), (0, LANES - n_experts)))), axis=1)
    br = jnp.pad(b_router, (0, LANES - n_experts)).reshape(1, LANES)
    full = lambda shp: pl.BlockSpec(shp, lambda bi, i: (0,) * len(shp))
    t = bsz * s
    return pl.pallas_call(
        functools.partial(_cross_ln_router_kernel, n_experts=n_experts),
        out_shape=(jax.ShapeDtypeStruct((t, d), F32), jax.ShapeDtypeStruct((t * SLAB, LANES), F32),
                   jax.ShapeDtypeStruct((t, LANES), jnp.int32), jax.ShapeDtypeStruct((t, LANES), F32)),
        grid=(bsz, nq),
        in_specs=[
            pl.BlockSpec((None, tq, d), lambda bi, i: (bi, i, 0)),
            pl.BlockSpec((None, m, kv.shape[2]), lambda bi, i: (bi, 0, 0)),
            full(wq_bf.shape), full(wo_bf.shape), full((1, d)), full((1, d)), full(wr.shape), full((1, LANES)),
        ],
        out_specs=(
            pl.BlockSpec((tq, d), lambda bi, i: (bi * nq + i, 0)),
            pl.BlockSpec((tq * SLAB, LANES), lambda bi, i: (bi * nq + i, 0)),
            pl.BlockSpec((tq, LANES), lambda bi, i: (bi * nq + i, 0)),
            pl.BlockSpec((tq, LANES), lambda bi, i: (bi * nq + i, 0)),
        ),
        compiler_params=_params(("arbitrary", "arbitrary")),
        name="cross_ln_router",
    )(x3d, kv, wq_bf, wo_bf, g.reshape(1, d), b.reshape(1, d), wr, br)


def _row_copy(src_hbm, src_row, dst, dst_row, sem, lay):
    return pltpu.make_async_copy(src_hbm.at[pl.ds(src_row * lay.rows, lay.rows), :],
                                 dst.at[pl.ds(dst_row * lay.pitch, lay.rows), :], sem)


ISSUE_UNROLL = 8


def _issue_rows(idx_ref, n, src_hbm, dst, sem, lay):
    def body(g, carry):
        r0 = g * ISSUE_UNROLL
        for u in range(ISSUE_UNROLL):
            _row_copy(src_hbm, idx_ref[0, r0 + u], dst, r0 + u, sem, lay).start(priority=u % 2)
        return carry

    lax.fori_loop(0, n // ISSUE_UNROLL, body, 0)


def _wait_rows(n, src_hbm, dst, sem, lay):
    pltpu.make_async_copy(src_hbm.at[pl.ds(0, n * lay.rows), :], dst.at[pl.ds(0, n * lay.rows), :], sem).wait()


ROW_CHUNK = 128


def _row_chunk(rows):
    return min(ROW_CHUNK, rows)


def _gather_ring(n_tiles, idx_refs, rows, src_hbm, slabs, sem, consume, lay, chunks_ref=None):
    depth = len(slabs)
    i = pl.program_id(0)

    def start(tile, idx_ref, slot):
        n = rows if chunks_ref is None else chunks_ref[tile] * _row_chunk(rows)
        _issue_rows(idx_ref, n, src_hbm, slabs[slot], sem.at[slot], lay)

    def finish(tile, slot):
        if chunks_ref is None:
            _wait_rows(rows, src_hbm, slabs[slot], sem.at[slot], lay)
        else:
            for q in range(rows // _row_chunk(rows)):
                @pl.when(q < chunks_ref[tile])
                def _():
                    _wait_rows(_row_chunk(rows), src_hbm, slabs[slot], sem.at[slot], lay)

    @pl.when(i == 0)
    def _():
        for d in range(min(depth - 1, n_tiles)):
            start(d, idx_refs[d], d)

    for s in range(depth):
        @pl.when(i % depth == s)
        def _(s=s):
            @pl.when(i + depth - 1 < n_tiles)
            def _():
                start(i + depth - 1, idx_refs[depth - 1], (s + depth - 1) % depth)

            finish(i, s)
            consume(slabs[s])


def _ring_index_specs(n_tiles, width, depth):
    return [pl.BlockSpec((None, 1, width), lambda i, d=d: (jnp.minimum(i + d, n_tiles - 1), 0, 0),
                         memory_space=pltpu.SMEM) for d in range(depth)]


DISPATCH_DEPTH = 3


def _gather_kernel(*refs, tm, n_tiles):
    chunks_ref = refs[0]
    tok_refs = refs[1:DISPATCH_DEPTH + 1]
    x_hbm, o_ref = refs[DISPATCH_DEPTH + 1:DISPATCH_DEPTH + 3]
    slabs = refs[DISPATCH_DEPTH + 3:2 * DISPATCH_DEPTH + 3]
    sem = refs[2 * DISPATCH_DEPTH + 3]

    @pl.when(pl.program_id(0) == 0)
    def _():
        for slab in slabs:
            slab[...] = jnp.zeros_like(slab)

    def relayout(slab):
        for c in range(SLAB):
            o_ref[:, c * LANES:(c + 1) * LANES] = slab[pl.ds(c, tm, stride=PITCH), :].astype(o_ref.dtype)

    _gather_ring(n_tiles, tok_refs, tm, x_hbm, slabs, sem, relayout, F32_ROW, chunks_ref)


def _dispatch_gather(row_tok, tile_chunks, x_slab, n_rows, tm, d):
    n_tiles = n_rows // tm
    tok3 = row_tok.reshape(n_tiles, 1, tm)
    return pl.pallas_call(
        functools.partial(_gather_kernel, tm=tm, n_tiles=n_tiles),
        out_shape=jax.ShapeDtypeStruct((n_rows, d), BF16),
        grid=(n_tiles,),
        in_specs=[pl.BlockSpec(memory_space=pltpu.SMEM)] + _ring_index_specs(n_tiles, tm, DISPATCH_DEPTH)
        + [pl.BlockSpec(memory_space=pl.ANY)],
        out_specs=pl.BlockSpec((tm, d), lambda i: (i, 0)),
        scratch_shapes=[pltpu.VMEM((tm * PITCH, LANES), F32) for _ in range(DISPATCH_DEPTH)]
        + [pltpu.SemaphoreType.DMA((DISPATCH_DEPTH,))],
        compiler_params=_params(("arbitrary",)),
        name="moe_dispatch",
    )(tile_chunks, *([tok3] * DISPATCH_DEPTH), x_slab)


def _expert_kernel(te_ref, tv_ref, x_ref, wg_ref, wu_ref, bg_ref, bu_ref, wd_ref, bd_ref, o_ref, acc_ref, *, tm):
    i = pl.program_id(0)
    f = pl.program_id(1)
    nf = pl.num_programs(1)

    @pl.when((tv_ref[i] == 0) & (f == nf - 1))
    def _():
        o_ref[...] = jnp.zeros_like(o_ref)

    @pl.when((i == 0) & (f == 0))
    def _():
        acc_ref[...] = jnp.zeros_like(acc_ref)

    @pl.when(tv_ref[i] > 0)
    def _():
        x = x_ref[...]
        gate = jnp.dot(x, wg_ref[...], preferred_element_type=F32) + bg_ref[...]
        up = jnp.dot(x, wu_ref[...], preferred_element_type=F32) + bu_ref[...]
        gate = jnp.minimum(gate, SWIGLU_LIMIT)
        up = jnp.clip(up, -SWIGLU_LIMIT, SWIGLU_LIMIT)
        act = (up + 1.0) * (gate * jax.nn.sigmoid(SWIGLU_ALPHA * gate))
        prev = jnp.where(f == 0, jnp.broadcast_to(bd_ref[...], acc_ref.shape), acc_ref[...])
        val = prev + jnp.dot(act.astype(BF16), wd_ref[...], preferred_element_type=F32)
        acc_ref[...] = val
        for c in range(PACK):
            lo = lax.bitcast_convert_type(val[:, c * LANES:(c + 1) * LANES].astype(BF16).astype(F32), jnp.uint32)
            hi = lax.bitcast_convert_type(val[:, (c + PACK) * LANES:(c + PACK + 1) * LANES].astype(BF16).astype(F32),
                                          jnp.uint32)
            o_ref[pl.ds(c, tm, stride=PACK), :] = (hi & jnp.uint32(0xFFFF0000)) | (lo >> 16)


def _expert_mlp(tile_expert, tile_valid, xs, wgu_bf, b_gu, wd_bf, b_down, layer, tm, tf):
    n_rows, d = xs.shape
    _, n_exp, _, two_ff = wgu_bf.shape
    d_ff = two_ff // 2
    nf = d_ff // tf
    wf = lambda i, f, tv: jnp.where(tv[i] > 0, f, nf - 1)
    return pl.pallas_call(
        functools.partial(_expert_kernel, tm=tm),
        out_shape=jax.ShapeDtypeStruct((n_rows * PACK, LANES), jnp.uint32),
        grid_spec=pltpu.PrefetchScalarGridSpec(
            num_scalar_prefetch=2,
            grid=(n_rows // tm, nf),
            in_specs=[
                pl.BlockSpec((tm, d), lambda i, f, te, tv: (i, 0)),
                pl.BlockSpec((None, None, d, tf), lambda i, f, te, tv: (layer, te[i], 0, wf(i, f, tv))),
                pl.BlockSpec((None, None, d, tf), lambda i, f, te, tv: (layer, te[i], 0, nf + wf(i, f, tv))),
                pl.BlockSpec((None, 1, tf), lambda i, f, te, tv: (te[i], 0, wf(i, f, tv))),
                pl.BlockSpec((None, 1, tf), lambda i, f, te, tv: (te[i], 0, nf + wf(i, f, tv))),
                pl.BlockSpec((None, None, tf, d), lambda i, f, te, tv: (layer, te[i], wf(i, f, tv), 0)),
                pl.BlockSpec((None, 1, d), lambda i, f, te, tv: (te[i], 0, 0)),
            ],
            out_specs=pl.BlockSpec((tm * PACK, LANES), lambda i, f, te, tv: (i, 0)),
            scratch_shapes=[pltpu.VMEM((tm, d), F32)],
        ),
        compiler_params=_params(("arbitrary", "arbitrary")),
        name="moe_experts",
    )(tile_expert, tile_valid, xs, wgu_bf, wgu_bf, b_gu.reshape(n_exp, 1, two_ff), b_gu.reshape(n_exp, 1, two_ff),
      wd_bf, b_down.reshape(n_exp, 1, d))


COMBINE_DEPTH = 2


def _combine_ln_kernel(*refs, tt, n_tiles):
    pos_refs = refs[:COMBINE_DEPTH]
    y_hbm, gate_ref, x_ref, g_ref, b_ref, o_ref = refs[COMBINE_DEPTH:COMBINE_DEPTH + 6]
    slabs = refs[COMBINE_DEPTH + 6:2 * COMBINE_DEPTH + 6]
    y_scr, sem = refs[2 * COMBINE_DEPTH + 6:]

    def weighted_sum(slab):
        gates = gate_ref[...]
        gk = [jnp.broadcast_to(gates[:, k:k + 1], (tt, LANES)) for k in range(TOPK)]
        for c in range(PACK):
            acc_lo = acc_hi = None
            for k in range(TOPK):
                w = slab[pl.ds(k * tt * PACK + c, tt, stride=PACK), :]
                lo = gk[k] * lax.bitcast_convert_type(w << 16, F32)
                hi = gk[k] * lax.bitcast_convert_type(w & jnp.uint32(0xFFFF0000), F32)
                acc_lo = lo if acc_lo is None else acc_lo + lo
                acc_hi = hi if acc_hi is None else acc_hi + hi
            y_scr[:, c * LANES:(c + 1) * LANES] = acc_lo
            y_scr[:, (c + PACK) * LANES:(c + PACK + 1) * LANES] = acc_hi

    _gather_ring(n_tiles, pos_refs, tt * TOPK, y_hbm, slabs, sem, weighted_sum, PACKED_ROW)
    o_ref[...] = _layer_norm_rows(DEEPNORM_ALPHA * x_ref[...] + y_scr[...], g_ref[...], b_ref[...])


def _combine_ln(pos, y_slab, gates, x2d, g, b):
    t, d = x2d.shape
    tt = 256
    n_tiles = t // tt
    pos3 = pos.reshape(n_tiles, tt, TOPK).transpose(0, 2, 1).reshape(n_tiles, 1, TOPK * tt)
    stage = pltpu.VMEM((tt * TOPK * PACKED_ROW.pitch, LANES), jnp.uint32)
    return pl.pallas_call(
        functools.partial(_combine_ln_kernel, tt=tt, n_tiles=n_tiles),
        out_shape=jax.ShapeDtypeStruct((t, d), F32),
        grid=(n_tiles,),
        in_specs=_ring_index_specs(n_tiles, tt * TOPK, COMBINE_DEPTH) + [
            pl.BlockSpec(memory_space=pl.ANY),
            pl.BlockSpec((tt, LANES), lambda i: (i, 0)),
            pl.BlockSpec((tt, d), lambda i: (i, 0)),
            pl.BlockSpec((1, d), lambda i: (0, 0)),
            pl.BlockSpec((1, d), lambda i: (0, 0)),
        ],
        out_specs=pl.BlockSpec((tt, d), lambda i: (i, 0)),
        scratch_shapes=[stage] * COMBINE_DEPTH + [pltpu.VMEM((tt, d), F32),
                                                  pltpu.SemaphoreType.DMA((COMBINE_DEPTH,))],
        compiler_params=_params(("arbitrary",)),
        name="moe_combine_ln",
    )(*([pos3] * COMBINE_DEPTH), y_slab, gates, x2d, g.reshape(1, d), b.reshape(1, d))


def _routing_tables(top_idx, n_experts, tm):
    t = top_idx.shape[0]
    n_assign = t * TOPK
    e_flat = top_idx.reshape(-1)
    onehot = (e_flat[:, None] == jnp.arange(n_experts, dtype=jnp.int32)[None, :]).astype(jnp.int32)
    csum = jnp.cumsum(onehot, axis=0)
    counts = csum[-1]
    rank = jnp.sum((csum - onehot) * onehot, axis=1)
    padded = (counts + tm - 1) // tm * tm
    p_end = jnp.cumsum(padded)
    p_start = p_end - padded
    start = jnp.cumsum(counts) - counts
    pos = (p_start[e_flat] + rank).astype(jnp.int32)
    n_tiles = (n_assign + n_experts * (tm - 1)) // tm
    n_rows = n_tiles * tm
    order = jnp.argsort(e_flat)
    tile_expert = jnp.minimum(jnp.searchsorted(p_end, jnp.arange(n_tiles) * tm, side="right"),
                              n_experts - 1).astype(jnp.int32)
    tile_valid = (jnp.arange(n_tiles) * tm < p_end[-1]).astype(jnp.int32)
    rows = jnp.arange(n_rows)
    row_e = jnp.repeat(tile_expert, tm)
    off = rows - p_start[row_e]
    real = (off < counts[row_e]) & (rows < p_end[-1])
    src = jnp.clip(start[row_e] + off, 0, n_assign - 1)
    row_tok = jnp.where(real, order[src] // TOPK, 0).astype(jnp.int32)
    tile_fill = jnp.clip(jnp.sum(real.reshape(n_tiles, tm), axis=1), 0, tm)
    chunk = _row_chunk(tm)
    tile_chunks = ((tile_fill + chunk - 1) // chunk).astype(jnp.int32)
    return row_tok, pos, tile_expert, tile_valid, tile_chunks, n_rows


MOE_TM = 512
MOE_TF = 1024


def kernel(x, mem, w_in, b_in, attn_sinks, w_out, ln1_g, ln1_b, w_mem_q, w_mem_k, w_mem_v, w_mem_o, ln2_g, ln2_b,
           w_router, b_router, w_gate_up, b_gate_up, w_down, b_down, ln3_g, ln3_b):
    bsz, s, d = x.shape
    t = bsz * s
    depth = w_in.shape[0]
    n_experts = w_router.shape[-1]
    mem2d = mem.reshape(bsz * mem.shape[1], d)
    x2d = x.reshape(t, d)
    wgu_bf = w_gate_up.astype(BF16)
    wd_bf = w_down.astype(BF16)
    tf = min(MOE_TF, w_down.shape[2])
    for i in range(depth):
        proj = _inproj(x2d, w_in[i].astype(BF16), b_in[i]).reshape(bsz, s, IN_WIDTH)
        oa = _swa_attention(proj, attn_sinks[i])
        ob = _moba_attention(proj)
        oc = _dilated_attention(proj)
        x1 = _outproj_ln(oa.reshape(t, A_Q_W), ob.reshape(t, B_W), oc.reshape(t, C_W), w_out[i].astype(BF16), x2d,
                         ln1_g[i], ln1_b[i])
        wkv = jnp.concatenate([w_mem_k[i], w_mem_v[i]], axis=1).astype(BF16)
        kv = _mem_kv(mem2d, wkv).reshape(bsz, mem.shape[1], wkv.shape[1])
        x2, x2_slab, top_idx, gates = _cross_ln_router(x1.reshape(bsz, s, d), kv, w_mem_q[i].astype(BF16),
                                                       w_mem_o[i].astype(BF16), ln2_g[i], ln2_b[i],
                                                       w_router[i], b_router[i])
        row_tok, pos, tile_expert, tile_valid, tile_chunks, n_rows = _routing_tables(top_idx[:, :TOPK], n_experts,
                                                                                     MOE_TM)
        xs = _dispatch_gather(row_tok, tile_chunks, x2_slab, n_rows, MOE_TM, d)
        y_slab = _expert_mlp(tile_expert, tile_valid, xs, wgu_bf, b_gate_up[i], wd_bf, b_down[i], i, MOE_TM, tf)
        x2d = _combine_ln(pos, y_slab, gates, x2, ln3_g[i], ln3_b[i])
    return x2d.reshape(bsz, s, d)
```

```python
import functools
from typing import NamedTuple

import numpy as np
import jax
import jax.numpy as jnp
from jax import lax
from jax.experimental import pallas as pl
from jax.experimental.pallas import tpu as pltpu

F32 = jnp.float32
BF16 = jnp.bfloat16
NEG_INF = float("-inf")
MASK_VALUE = -1e30

LANES = 128
SUBLANES = 8
VMEM_LIMIT_BYTES = 56 * 1024 * 1024

HEAD_DIM = 64
A_Q_HEADS = 16
A_KV_HEADS = 2
A_GROUP = A_Q_HEADS // A_KV_HEADS
A_WINDOW = 128
B_HEADS = 8
MOBA_BLOCK = 256
MOBA_TOPK = 3
C_HEADS = 8
DILATED_PAIRS = ((128, 1), (512, 4), (2048, 16))
BAND = 128
A_Q_W = A_Q_HEADS * HEAD_DIM
A_KV_W = A_KV_HEADS * HEAD_DIM
B_W = B_HEADS * HEAD_DIM
C_W = C_HEADS * HEAD_DIM
IN_WIDTH = A_Q_W + 2 * A_KV_W + 3 * B_W + 3 * C_W
COL_AK = A_Q_W // LANES
COL_AV = COL_AK + A_KV_W // LANES
COL_BQ = COL_AV + A_KV_W // LANES
COL_BK = COL_BQ + B_W // LANES
COL_BV = COL_BK + B_W // LANES
COL_CQ = COL_BV + B_W // LANES
COL_CK = COL_CQ + C_W // LANES
COL_CV = COL_CK + C_W // LANES
MEM_HEADS = 4
MEM_HEAD_DIM = 128
TOPK = 4
SWIGLU_LIMIT = 7.0
SWIGLU_ALPHA = 1.702
LN_EPS = 1e-5
DEPTH = 2
DEEPNORM_ALPHA = (2 * DEPTH) ** 0.25
PACK = 8


class RowLayout(NamedTuple):
    rows: int
    pitch: int


PACKED_ROW = RowLayout(PACK, PACK)
HIGH_HALF = 0xFFFF0000


def _pack_bf16_pair(lo, hi):
    lo_bits = lax.bitcast_convert_type(lo.astype(BF16).astype(F32), jnp.uint32)
    hi_bits = lax.bitcast_convert_type(hi.astype(BF16).astype(F32), jnp.uint32)
    return (hi_bits & jnp.uint32(HIGH_HALF)) | (lo_bits >> 16)


def _unpack_bf16_pair(w):
    return (lax.bitcast_convert_type(w << 16, F32), lax.bitcast_convert_type(w & jnp.uint32(HIGH_HALF), F32))


def _store_packed_rows(ref, val, n):
    for c in range(PACK):
        ref[pl.ds(c, n, stride=PACK), :] = _pack_bf16_pair(val[:, c * LANES:(c + 1) * LANES],
                                                           val[:, (c + PACK) * LANES:(c + PACK + 1) * LANES])


def _alibi_slopes(n):
    return [float(2.0 ** (-8.0 * (i + 1) / n)) for i in range(n)]


def _params(sem=None):
    return pltpu.CompilerParams(dimension_semantics=sem, vmem_limit_bytes=VMEM_LIMIT_BYTES)


def _layer_norm_rows(y, g, b):
    mu = jnp.mean(y, axis=-1, keepdims=True)
    d = y - mu
    var = jnp.mean(d * d, axis=-1, keepdims=True)
    return d * lax.rsqrt(var + LN_EPS) * g + b


def _dot_nt(a, b):
    return lax.dot_general(a, b, (((1,), (1,)), ((), ())), preferred_element_type=F32)


def _inproj_kernel(x_ref, w_ref, b_ref, o_ref):
    acc = jnp.dot(x_ref[...].astype(BF16), w_ref[...], preferred_element_type=F32)
    o_ref[...] = acc + b_ref[...]


def _inproj(x2d, w_bf, b):
    t, d = x2d.shape
    n = w_bf.shape[1]
    tm, tn = 512, n // 2
    return pl.pallas_call(
        _inproj_kernel,
        out_shape=jax.ShapeDtypeStruct((t, n), F32),
        grid=(n // tn, t // tm),
        in_specs=[
            pl.BlockSpec((tm, d), lambda j, i: (i, 0)),
            pl.BlockSpec((d, tn), lambda j, i: (0, j)),
            pl.BlockSpec((1, tn), lambda j, i: (0, j)),
        ],
        out_specs=pl.BlockSpec((tm, tn), lambda j, i: (i, j)),
        compiler_params=_params(("arbitrary", "arbitrary")),
        name="inproj",
    )(x2d, w_bf, b.reshape(1, n))


def _swa_kernel(sink_ref, q_ref, kp_ref, kc_ref, vp_ref, vc_ref, o_ref, bias_ref):
    first_step = (pl.program_id(0) == 0) & (pl.program_id(1) == 0)
    i = pl.program_id(1)
    ki = lax.broadcasted_iota(jnp.int32, (2 * BAND, BAND), 0)
    qi = lax.broadcasted_iota(jnp.int32, (2 * BAND, BAND), 1)
    slopes = _alibi_slopes(A_Q_HEADS)

    @pl.when(first_step)
    def _():
        dist = BAND + qi - ki
        in_band = (dist >= 0) & (dist <= A_WINDOW - 1)
        distf = dist.astype(F32)
        for h in range(A_Q_HEADS):
            bias_ref[h] = jnp.where(in_band, -slopes[h] * distf, MASK_VALUE)

    fix = jnp.where((i > 0) | (ki >= BAND), 0.0, MASK_VALUE)
    q = q_ref[0]
    k2 = jnp.concatenate([kp_ref[0], kc_ref[0]], axis=0).astype(BF16)
    v2t = jnp.concatenate([vp_ref[0], vc_ref[0]], axis=0).T
    row_v = lax.broadcasted_iota(jnp.int32, (HEAD_DIM, 2 * BAND), 0)
    ones_row = jnp.where(row_v == 0, 1.0, 0.0)
    ss, sinks = [], []
    for hk in range(A_KV_HEADS):
        heads = range(hk * A_GROUP, (hk + 1) * A_GROUP)
        kv = slice(hk * HEAD_DIM, (hk + 1) * HEAD_DIM)
        q_grp = jnp.concatenate([q[:, h * HEAD_DIM:(h + 1) * HEAD_DIM] for h in heads], axis=0)
        q_grp = (q_grp * (HEAD_DIM ** -0.5)).astype(BF16)
        s = _dot_nt(k2[:, kv], q_grp)
        ss.append(s + jnp.concatenate([bias_ref[h] + fix for h in heads], axis=1))
        sinks.append(jnp.concatenate([jnp.full((1, BAND), sink_ref[h], F32) for h in heads], axis=1))
    ms = [jnp.maximum(jnp.max(s, axis=0, keepdims=True), sk) for s, sk in zip(ss, sinks)]
    es = [jnp.exp(s - m).astype(BF16) for s, m in zip(ss, ms)]
    for hk in range(A_KV_HEADS):
        kv = slice(hk * HEAD_DIM, (hk + 1) * HEAD_DIM)
        vaug = jnp.concatenate([v2t[kv, :], ones_row], axis=0).astype(BF16)
        o = jnp.dot(vaug, es[hk], preferred_element_type=F32)
        den = o[HEAD_DIM:HEAD_DIM + 1, :] + jnp.exp(sinks[hk] - ms[hk])
        o = o[:HEAD_DIM, :] / den
        for g in range(0, A_GROUP, 2):
            pair = jnp.concatenate([o[:, g * BAND:(g + 1) * BAND], o[:, (g + 1) * BAND:(g + 2) * BAND]], axis=0)
            col = (hk * A_GROUP + g) * HEAD_DIM
            o_ref[0, :, col:col + 2 * HEAD_DIM] = pair.T.astype(o_ref.dtype)


def _swa_attention(proj, sinks):
    b, s, _ = proj.shape
    nblk = s // BAND
    prev = lambda i: jnp.maximum(i - 1, 0)
    return pl.pallas_call(
        _swa_kernel,
        out_shape=jax.ShapeDtypeStruct((b, s, A_Q_W), BF16),
        grid=(b, nblk),
        in_specs=[
            pl.BlockSpec(memory_space=pltpu.SMEM),
            pl.BlockSpec((1, BAND, A_Q_W), lambda bi, i: (bi, i, 0)),
            pl.BlockSpec((1, BAND, LANES), lambda bi, i: (bi, prev(i), COL_AK)),
            pl.BlockSpec((1, BAND, LANES), lambda bi, i: (bi, i, COL_AK)),
            pl.BlockSpec((1, BAND, LANES), lambda bi, i: (bi, prev(i), COL_AV)),
            pl.BlockSpec((1, BAND, LANES), lambda bi, i: (bi, i, COL_AV)),
        ],
        out_specs=pl.BlockSpec((1, BAND, A_Q_W), lambda bi, i: (bi, i, 0)),
        scratch_shapes=[pltpu.VMEM((A_Q_HEADS, 2 * BAND, BAND), F32)],
        compiler_params=_params(("arbitrary", "arbitrary")),
        name="swa_attention",
    )(sinks, proj, proj, proj, proj, proj)


PEN_COLS = 16


def _moba_kernel(slope_ref, q_ref, k_ref, v_ref, o_ref, kmean_ref, kaug_ref, vaug_ref, *, nb):
    hp = pl.program_id(1)
    j = pl.program_id(2)
    blk = MOBA_BLOCK
    nh = LANES // HEAD_DIM

    @pl.when(j == 0)
    def _():
        kmean_ref[...] = jnp.zeros_like(kmean_ref)
        lane_k = lax.broadcasted_iota(jnp.int32, (blk, LANES), 1)
        row_v = lax.broadcasted_iota(jnp.int32, (HEAD_DIM, blk), 0)
        ones_row = jnp.where(row_v == 0, 1.0, 0.0)
        for n in range(nb):
            kblk = k_ref[n * blk:(n + 1) * blk, :]
            kmean_ref[n:n + 1, :] = jnp.sum(kblk, axis=0, keepdims=True) * (1.0 / blk)
            onehot = jnp.where((lane_k == HEAD_DIM + n) | (lane_k == HEAD_DIM + PEN_COLS + n), 1.0, 0.0)
            vt = v_ref[n * blk:(n + 1) * blk, :].T
            for hh in range(nh):
                sl = slice(hh * HEAD_DIM, (hh + 1) * HEAD_DIM)
                k_low = kblk if hh == 0 else pltpu.roll(kblk, HEAD_DIM, 1)
                kaug_ref[hh, n] = jnp.where(lane_k < HEAD_DIM, k_low, onehot).astype(BF16)
                vaug_ref[hh, n] = jnp.concatenate([vt[sl, :], ones_row], axis=0).astype(BF16)

    ki = lax.broadcasted_iota(jnp.int32, (blk, blk), 0)
    qi = lax.broadcasted_iota(jnp.int32, (blk, blk), 1)
    rel = (qi - ki).astype(F32)
    causal = qi >= ki
    blk_id = lax.broadcasted_iota(jnp.int32, (PEN_COLS, blk), 0)
    qt_both = q_ref[...].T
    head_sl = [slice(hh * HEAD_DIM, (hh + 1) * HEAD_DIM) for hh in range(nh)]
    gates = [lax.dot_general(kmean_ref[:PEN_COLS, sl], q_ref[:, sl], (((1,), (1,)), ((), ())),
                             precision=lax.Precision.HIGHEST, preferred_element_type=F32) for sl in head_sl]
    qt_aug, qt_own, bias = [], [], []
    for hh in range(nh):
        sl = head_sl[hh]
        slope = slope_ref[hp * nh + hh]
        g = jnp.where(blk_id < j, gates[hh], NEG_INF)
        sel = jnp.zeros((PEN_COLS, blk), jnp.bool_)
        for _ in range(MOBA_TOPK):
            m = jnp.max(g, axis=0, keepdims=True)
            idx = jnp.min(jnp.where(g == m, blk_id, PEN_COLS), axis=0, keepdims=True)
            hit = blk_id == idx
            sel = sel | (hit & (m > NEG_INF))
            g = jnp.where(hit, NEG_INF, g)
        block_off = -slope * ((j - blk_id) * blk).astype(F32)
        pen = jnp.where(sel, block_off, MASK_VALUE)
        pen_hi = pen.astype(BF16)
        pen_lo = (pen - pen_hi.astype(F32)).astype(BF16)
        qt = (qt_both[sl, :] * (HEAD_DIM ** -0.5)).astype(BF16)
        qt_aug.append(jnp.concatenate([qt, pen_hi, pen_lo, jnp.zeros((HEAD_DIM - 2 * PEN_COLS, blk), BF16)], axis=0))
        qt_own.append(jnp.concatenate([qt, jnp.zeros((HEAD_DIM, blk), BF16)], axis=0))
        bias.append(-slope * rel)

    def scores(n, qs):
        return tuple(jnp.dot(kaug_ref[hh, n], qs[hh], preferred_element_type=F32) + bias[hh] for hh in range(nh))

    def pair_blocks(t):
        return jnp.minimum(2 * t, nb - 1), jnp.minimum(2 * t + 1, nb - 1)

    def pair_scores(t):
        return tuple(scores(n, qt_aug) for n in pair_blocks(t))

    s_own = [jnp.where(causal, s, MASK_VALUE) for s in scores(j, qt_own)]
    first = pair_scores(0)
    m0 = [jnp.max(s, axis=0, keepdims=True) for s in s_own]
    p0 = [jnp.exp(s - m).astype(BF16) for s, m in zip(s_own, m0)]
    o0 = [jnp.dot(vaug_ref[hh, j], p0[hh], preferred_element_type=F32) for hh in range(nh)]

    def body(t, carry):
        ms, os_, s_cur = carry
        blocks = pair_blocks(t)
        s_next = pair_scores(t + 1)
        new_m, new_o = [], []
        for hh in range(nh):
            m_new = ms[hh]
            for sb in s_cur:
                m_new = jnp.maximum(m_new, jnp.max(sb[hh], axis=0, keepdims=True))
            acc = jnp.exp(ms[hh] - m_new) * os_[hh]
            for n, sb in zip(blocks, s_cur):
                acc += jnp.dot(vaug_ref[hh, n], jnp.exp(sb[hh] - m_new).astype(BF16), preferred_element_type=F32)
            new_m.append(m_new)
            new_o.append(acc)
        return tuple(new_m), tuple(new_o), s_next

    _, o_fin, _ = lax.fori_loop(0, (j + 1) // 2, body, (tuple(m0), tuple(o0), first))
    outs = [o[:HEAD_DIM, :] / o[HEAD_DIM:HEAD_DIM + 1, :] for o in o_fin]
    o_ref[...] = jnp.concatenate(outs, axis=0).T.astype(o_ref.dtype)


def _moba_attention(proj):
    b, s, _ = proj.shape
    nb = s // MOBA_BLOCK
    assert nb <= PEN_COLS
    npair = B_W // LANES
    nh = LANES // HEAD_DIM
    slopes = jnp.asarray(_alibi_slopes(B_HEADS), F32)
    return pl.pallas_call(
        functools.partial(_moba_kernel, nb=nb),
        out_shape=jax.ShapeDtypeStruct((b, s, B_W), BF16),
        grid=(b, npair, nb),
        in_specs=[
            pl.BlockSpec(memory_space=pltpu.SMEM),
            pl.BlockSpec((None, MOBA_BLOCK, LANES), lambda bi, hp, j: (bi, j, COL_BQ + hp)),
            pl.BlockSpec((None, s, LANES), lambda bi, hp, j: (bi, 0, COL_BK + hp)),
            pl.BlockSpec((None, s, LANES), lambda bi, hp, j: (bi, 0, COL_BV + hp)),
        ],
        out_specs=pl.BlockSpec((None, MOBA_BLOCK, LANES), lambda bi, hp, j: (bi, j, hp)),
        scratch_shapes=[pltpu.VMEM((LANES, LANES), F32),
                        pltpu.VMEM((nh, nb, MOBA_BLOCK, LANES), BF16),
                        pltpu.VMEM((nh, nb, LANES, MOBA_BLOCK), BF16)],
        compiler_params=_params(("arbitrary", "arbitrary", "arbitrary")),
        name="moba_attention",
    )(slopes, proj, proj, proj)


DIL_GROUP = 4


def _dilated_kernel(slope_ref, q_ref, k_ref, v_ref, o_ref, o0, o1, o2, e0, e1, e2, *, seq):
    hp = pl.program_id(1)
    o_scr = (o0, o1, o2)
    lse_scr = (e0, e1, e2)
    nh = LANES // HEAD_DIM
    ki = lax.broadcasted_iota(jnp.int32, (2 * BAND, BAND), 0)
    qi = lax.broadcasted_iota(jnp.int32, (2 * BAND, BAND), 1)
    dist = BAND + qi - ki
    distf = dist.astype(F32)
    first_fix = jnp.where(ki >= BAND, 0.0, MASK_VALUE)
    row_v = lax.broadcasted_iota(jnp.int32, (HEAD_DIM, 2 * BAND), 0)
    ones_row = jnp.where(row_v == 0, 1.0, 0.0)

    for p, (window, r) in enumerate(DILATED_PAIRS):
        nblk = seq // (r * BAND)
        in_band = (dist >= 0) & (dist <= window // r)
        biases = [jnp.where(in_band, -(slope_ref[hp * nh + hh] * float(r)) * distf, MASK_VALUE) for hh in range(nh)]

        def group(t, carry, p=p, r=r, biases=biases):
            rows, vts, ss = [], [], []
            for gidx in range(DIL_GROUP):
                tb = t * DIL_GROUP + gidx
                c = tb % r
                i = tb // r
                cur = c + i * (BAND * r)
                prv = c + jnp.maximum(i - 1, 0) * (BAND * r)
                rows_cur = pl.ds(cur, BAND, stride=r) if r > 1 else pl.ds(cur, BAND)
                rows_prv = pl.ds(prv, BAND, stride=r) if r > 1 else pl.ds(prv, BAND)
                q = (q_ref[rows_cur, :] * (HEAD_DIM ** -0.5)).astype(BF16)
                k2 = jnp.concatenate([k_ref[rows_prv, :], k_ref[rows_cur, :]], axis=0).astype(BF16)
                fix = jnp.where(i > 0, 0.0, first_fix)
                for hh in range(nh):
                    sl = slice(hh * HEAD_DIM, (hh + 1) * HEAD_DIM)
                    ss.append(_dot_nt(k2[:, sl], q[:, sl]) + biases[hh] + fix)
                rows.append(rows_cur)
                vts.append(jnp.concatenate([v_ref[rows_prv, :], v_ref[rows_cur, :]], axis=0).T)
            ms = [jnp.max(sc, axis=0, keepdims=True) for sc in ss]
            es = [jnp.exp(sc - m).astype(BF16) for sc, m in zip(ss, ms)]
            os_ = []
            for gidx in range(DIL_GROUP):
                for hh in range(nh):
                    sl = slice(hh * HEAD_DIM, (hh + 1) * HEAD_DIM)
                    vaug = jnp.concatenate([vts[gidx][sl, :], ones_row], axis=0).astype(BF16)
                    os_.append(jnp.dot(vaug, es[gidx * nh + hh], preferred_element_type=F32))
            for gidx in range(DIL_GROUP):
                outs, lses = [], []
                for hh in range(nh):
                    o = os_[gidx * nh + hh]
                    l = o[HEAD_DIM:HEAD_DIM + 1, :]
                    outs.append(o[:HEAD_DIM, :] / l)
                    lses.append(jnp.broadcast_to(ms[gidx * nh + hh] + jnp.log(l), (HEAD_DIM, BAND)))
                o_scr[p][rows[gidx], :] = jnp.concatenate(outs, axis=0).T
                lse_scr[p][rows[gidx], :] = jnp.concatenate(lses, axis=0).T
            return carry

        lax.fori_loop(0, r * nblk // DIL_GROUP, group, 0)

    chunk = 512

    def combine(t, carry):
        rows = pl.ds(pl.multiple_of(t * chunk, chunk), chunk)
        ls = [e[rows, :] for e in lse_scr]
        m = jnp.maximum(jnp.maximum(ls[0], ls[1]), ls[2])
        ws = [jnp.exp(l - m) for l in ls]
        den = ws[0] + ws[1] + ws[2]
        num = ws[0] * o_scr[0][rows, :] + ws[1] * o_scr[1][rows, :] + ws[2] * o_scr[2][rows, :]
        o_ref[rows, :] = (num / den).astype(o_ref.dtype)
        return carry

    lax.fori_loop(0, seq // chunk, combine, 0)


def _dilated_attention(proj):
    b, s, _ = proj.shape
    npair = C_W // LANES
    slopes = jnp.asarray(_alibi_slopes(C_HEADS), F32)
    spec = lambda col: pl.BlockSpec((None, s, LANES), lambda bi, hp: (bi, 0, col + hp))
    return pl.pallas_call(
        functools.partial(_dilated_kernel, seq=s),
        out_shape=jax.ShapeDtypeStruct((b, s, C_W), BF16),
        grid=(b, npair),
        in_specs=[pl.BlockSpec(memory_space=pltpu.SMEM), spec(COL_CQ), spec(COL_CK), spec(COL_CV)],
        out_specs=pl.BlockSpec((None, s, LANES), lambda bi, hp: (bi, 0, hp)),
        scratch_shapes=[pltpu.VMEM((s, LANES), F32) for _ in range(6)],
        compiler_params=_params(("arbitrary", "arbitrary")),
        name="dilated_attention",
    )(slopes, proj, proj, proj)


def _outproj_ln_kernel(oa_ref, ob_ref, oc_ref, w_ref, x_ref, g_ref, b_ref, o_ref):
    acc = jnp.dot(oa_ref[...], w_ref[0:A_Q_W, :], preferred_element_type=F32)
    acc += jnp.dot(ob_ref[...], w_ref[A_Q_W:A_Q_W + B_W, :], preferred_element_type=F32)
    acc += jnp.dot(oc_ref[...], w_ref[A_Q_W + B_W:, :], preferred_element_type=F32)
    o_ref[...] = _layer_norm_rows(DEEPNORM_ALPHA * x_ref[...] + acc, g_ref[...], b_ref[...])


def _outproj_ln(oa, ob, oc, w_bf, x2d, g, b):
    t, d = x2d.shape
    tm = 512
    row = lambda w: pl.BlockSpec((tm, w), lambda i: (i, 0))
    full = lambda shp: pl.BlockSpec(shp, lambda i: (0, 0))
    return pl.pallas_call(
        _outproj_ln_kernel,
        out_shape=jax.ShapeDtypeStruct((t, d), F32),
        grid=(t // tm,),
        in_specs=[row(A_Q_W), row(B_W), row(C_W), full(w_bf.shape), row(d), full((1, d)), full((1, d))],
        out_specs=row(d),
        compiler_params=_params(("arbitrary",)),
        name="outproj_ln",
    )(oa, ob, oc, w_bf, x2d, g.reshape(1, d), b.reshape(1, d))


def _memkv_kernel(m_ref, w_ref, o_ref):
    o_ref[...] = jnp.dot(m_ref[...].astype(BF16), w_ref[...], preferred_element_type=F32).astype(o_ref.dtype)


def _mem_kv(mem2d, wkv_bf):
    t, d = mem2d.shape
    n = wkv_bf.shape[1]
    tm = min(512, t)
    return pl.pallas_call(
        _memkv_kernel,
        out_shape=jax.ShapeDtypeStruct((t, n), BF16),
        grid=(t // tm,),
        in_specs=[pl.BlockSpec((tm, d), lambda i: (i, 0)), pl.BlockSpec((d, n), lambda i: (0, 0))],
        out_specs=pl.BlockSpec((tm, n), lambda i: (i, 0)),
        compiler_params=_params(("arbitrary",)),
        name="mem_kv",
    )(mem2d, wkv_bf)


def _split_bf16(a):
    hi = a.astype(BF16)
    lo = (a - hi.astype(F32)).astype(BF16)
    return hi, lo


def _cross_ln_router_kernel(x_ref, kv_ref, wq_ref, wo_ref, g_ref, b_ref, wr_ref, br_ref,
                            x2_ref, slab_ref, idx_ref, gate_ref, *, n_experts):
    x = x_ref[...]
    tq = x.shape[0]
    mw = MEM_HEADS * MEM_HEAD_DIM
    q = jnp.dot(x.astype(BF16), wq_ref[...], preferred_element_type=F32).astype(BF16)
    heads = []
    for h in range(MEM_HEADS):
        sl = slice(h * MEM_HEAD_DIM, (h + 1) * MEM_HEAD_DIM)
        s = _dot_nt(q[:, sl], kv_ref[:, sl]) * (MEM_HEAD_DIM ** -0.5)
        m = jnp.max(s, axis=-1, keepdims=True)
        e = jnp.exp(s - m)
        l = jnp.sum(e, axis=-1, keepdims=True)
        o = jnp.dot(e.astype(BF16), kv_ref[:, mw + h * MEM_HEAD_DIM:mw + (h + 1) * MEM_HEAD_DIM],
                    preferred_element_type=F32) / l
        heads.append(o.astype(BF16))
    o = jnp.concatenate(heads, axis=1)
    y = jnp.dot(o, wo_ref[...], preferred_element_type=F32)
    x2 = _layer_norm_rows(DEEPNORM_ALPHA * x + y, g_ref[...], b_ref[...])
    x2_ref[...] = x2
    _store_packed_rows(slab_ref, x2, tq)

    xh, xl = _split_bf16(x2)
    hi_both = jnp.dot(xh, wr_ref[...], preferred_element_type=F32)
    logits = (hi_both[:, :LANES] + hi_both[:, LANES:]
              + jnp.dot(xl, wr_ref[:, :LANES], preferred_element_type=F32)) + br_ref[...]
    lane = lax.broadcasted_iota(jnp.int32, (tq, LANES), 1)
    g = jnp.where(lane < n_experts, logits, NEG_INF)
    idx_out = jnp.zeros((tq, LANES), jnp.int32)
    val_out = jnp.zeros((tq, LANES), F32)
    top = None
    den = jnp.zeros((tq, 1), F32)
    for r in range(TOPK):
        m = jnp.max(g, axis=-1, keepdims=True)
        idx = jnp.min(jnp.where(g == m, lane, LANES), axis=-1, keepdims=True)
        g = jnp.where(lane == idx, NEG_INF, g)
        top = m if top is None else top
        e = jnp.exp(m - top)
        den = den + e
        idx_out = jnp.where(lane == r, idx, idx_out)
        val_out = jnp.where(lane == r, e, val_out)
    idx_ref[...] = idx_out
    gate_ref[...] = val_out / den


def _cross_ln_router(x3d, kv, wq_bf, wo_bf, g, b, w_router, b_router):
    bsz, s, d = x3d.shape
    m = kv.shape[1]
    n_experts = w_router.shape[1]
    tq = 512
    nq = s // tq
    wr = jnp.concatenate(_split_bf16(jnp.pad(w_router, ((0, 0), (0, LANES - n_experts)))), axis=1)
    br = jnp.pad(b_router, (0, LANES - n_experts)).reshape(1, LANES)
    full = lambda shp: pl.BlockSpec(shp, lambda bi, i: (0,) * len(shp))
    t = bsz * s
    return pl.pallas_call(
        functools.partial(_cross_ln_router_kernel, n_experts=n_experts),
        out_shape=(jax.ShapeDtypeStruct((t, d), F32), jax.ShapeDtypeStruct((t * PACK, LANES), jnp.uint32),
                   jax.ShapeDtypeStruct((t, LANES), jnp.int32), jax.ShapeDtypeStruct((t, LANES), F32)),
        grid=(bsz, nq),
        in_specs=[
            pl.BlockSpec((None, tq, d), lambda bi, i: (bi, i, 0)),
            pl.BlockSpec((None, m, kv.shape[2]), lambda bi, i: (bi, 0, 0)),
            full(wq_bf.shape), full(wo_bf.shape), full((1, d)), full((1, d)), full(wr.shape), full((1, LANES)),
        ],
        out_specs=(
            pl.BlockSpec((tq, d), lambda bi, i: (bi * nq + i, 0)),
            pl.BlockSpec((tq * PACK, LANES), lambda bi, i: (bi * nq + i, 0)),
            pl.BlockSpec((tq, LANES), lambda bi, i: (bi * nq + i, 0)),
            pl.BlockSpec((tq, LANES), lambda bi, i: (bi * nq + i, 0)),
        ),
        compiler_params=_params(("arbitrary", "arbitrary")),
        name="cross_ln_router",
    )(x3d, kv, wq_bf, wo_bf, g.reshape(1, d), b.reshape(1, d), wr, br)


def _row_copy(src_hbm, src_row, dst, dst_row, sem, lay):
    return pltpu.make_async_copy(src_hbm.at[pl.ds(src_row * lay.rows, lay.rows), :],
                                 dst.at[pl.ds(dst_row * lay.pitch, lay.rows), :], sem)


ISSUE_UNROLL = 8


def _issue_rows(idx_ref, n, src_hbm, dst, sem, lay):
    def body(g, carry):
        r0 = g * ISSUE_UNROLL
        for u in range(ISSUE_UNROLL):
            _row_copy(src_hbm, idx_ref[0, r0 + u], dst, r0 + u, sem, lay).start(priority=u % 2)
        return carry

    lax.fori_loop(0, n // ISSUE_UNROLL, body, 0)


def _wait_rows(n, src_hbm, dst, sem, lay):
    pltpu.make_async_copy(src_hbm.at[pl.ds(0, n * lay.rows), :], dst.at[pl.ds(0, n * lay.rows), :], sem).wait()


ROW_CHUNK = 128


def _row_chunk(rows):
    return min(ROW_CHUNK, rows)


def _gather_ring(n_tiles, idx_refs, rows, src_hbm, slabs, sem, consume, lay, chunks_ref=None):
    depth = len(slabs)
    i = pl.program_id(0)

    def start(tile, idx_ref, slot):
        n = rows if chunks_ref is None else chunks_ref[tile] * _row_chunk(rows)
        _issue_rows(idx_ref, n, src_hbm, slabs[slot], sem.at[slot], lay)

    def finish(tile, slot):
        if chunks_ref is None:
            _wait_rows(rows, src_hbm, slabs[slot], sem.at[slot], lay)
        else:
            for q in range(rows // _row_chunk(rows)):
                @pl.when(q < chunks_ref[tile])
                def _():
                    _wait_rows(_row_chunk(rows), src_hbm, slabs[slot], sem.at[slot], lay)

    @pl.when(i == 0)
    def _():
        for d in range(min(depth - 1, n_tiles)):
            start(d, idx_refs[d], d)

    for s in range(depth):
        @pl.when(i % depth == s)
        def _(s=s):
            @pl.when(i + depth - 1 < n_tiles)
            def _():
                start(i + depth - 1, idx_refs[depth - 1], (s + depth - 1) % depth)

            finish(i, s)
            consume(slabs[s])


def _ring_index_specs(n_tiles, width, depth):
    return [pl.BlockSpec((None, 1, width), lambda i, d=d: (jnp.minimum(i + d, n_tiles - 1), 0, 0),
                         memory_space=pltpu.SMEM) for d in range(depth)]


DISPATCH_DEPTH = 3


def _gather_kernel(*refs, tm, n_tiles):
    chunks_ref = refs[0]
    tok_refs = refs[1:DISPATCH_DEPTH + 1]
    x_hbm, o_ref = refs[DISPATCH_DEPTH + 1:DISPATCH_DEPTH + 3]
    slabs = refs[DISPATCH_DEPTH + 3:2 * DISPATCH_DEPTH + 3]
    sem = refs[2 * DISPATCH_DEPTH + 3]

    @pl.when(pl.program_id(0) == 0)
    def _():
        for slab in slabs:
            slab[...] = jnp.zeros_like(slab)

    def relayout(slab):
        for c in range(PACK):
            lo, hi = _unpack_bf16_pair(slab[pl.ds(c, tm, stride=PACK), :])
            o_ref[:, c * LANES:(c + 1) * LANES] = lo.astype(o_ref.dtype)
            o_ref[:, (c + PACK) * LANES:(c + PACK + 1) * LANES] = hi.astype(o_ref.dtype)

    _gather_ring(n_tiles, tok_refs, tm, x_hbm, slabs, sem, relayout, PACKED_ROW, chunks_ref)


def _dispatch_gather(row_tok, tile_chunks, x_slab, n_rows, tm, d):
    n_tiles = n_rows // tm
    tok3 = row_tok.reshape(n_tiles, 1, tm)
    return pl.pallas_call(
        functools.partial(_gather_kernel, tm=tm, n_tiles=n_tiles),
        out_shape=jax.ShapeDtypeStruct((n_rows, d), BF16),
        grid=(n_tiles,),
        in_specs=[pl.BlockSpec(memory_space=pltpu.SMEM)] + _ring_index_specs(n_tiles, tm, DISPATCH_DEPTH)
        + [pl.BlockSpec(memory_space=pl.ANY)],
        out_specs=pl.BlockSpec((tm, d), lambda i: (i, 0)),
        scratch_shapes=[pltpu.VMEM((tm * PACKED_ROW.pitch, LANES), jnp.uint32) for _ in range(DISPATCH_DEPTH)]
        + [pltpu.SemaphoreType.DMA((DISPATCH_DEPTH,))],
        compiler_params=_params(("arbitrary",)),
        name="moe_dispatch",
    )(tile_chunks, *([tok3] * DISPATCH_DEPTH), x_slab)


def _expert_kernel(te_ref, tv_ref, x_ref, wg_ref, wu_ref, bg_ref, bu_ref, wd_ref, bd_ref, o_ref, acc_ref, *, tm):
    i = pl.program_id(0)
    f = pl.program_id(1)
    nf = pl.num_programs(1)

    @pl.when((tv_ref[i] == 0) & (f == nf - 1))
    def _():
        o_ref[...] = jnp.zeros_like(o_ref)

    @pl.when((i == 0) & (f == 0))
    def _():
        acc_ref[...] = jnp.zeros_like(acc_ref)

    @pl.when(tv_ref[i] > 0)
    def _():
        x = x_ref[...]
        gate = jnp.dot(x, wg_ref[...], preferred_element_type=F32) + bg_ref[...]
        up = jnp.dot(x, wu_ref[...], preferred_element_type=F32) + bu_ref[...]
        gate = jnp.minimum(gate, SWIGLU_LIMIT)
        up = jnp.clip(up, -SWIGLU_LIMIT, SWIGLU_LIMIT)
        act = (up + 1.0) * (gate * jax.nn.sigmoid(SWIGLU_ALPHA * gate))
        prev = jnp.where(f == 0, jnp.broadcast_to(bd_ref[...], acc_ref.shape), acc_ref[...])
        val = prev + jnp.dot(act.astype(BF16), wd_ref[...], preferred_element_type=F32)
        acc_ref[...] = val
        _store_packed_rows(o_ref, val, tm)


def _expert_mlp(tile_expert, tile_valid, xs, wgu_bf, b_gu, wd_bf, b_down, layer, tm, tf):
    n_rows, d = xs.shape
    _, n_exp, _, two_ff = wgu_bf.shape
    d_ff = two_ff // 2
    nf = d_ff // tf
    wf = lambda i, f, tv: jnp.where(tv[i] > 0, f, nf - 1)
    return pl.pallas_call(
        functools.partial(_expert_kernel, tm=tm),
        out_shape=jax.ShapeDtypeStruct((n_rows * PACK, LANES), jnp.uint32),
        grid_spec=pltpu.PrefetchScalarGridSpec(
            num_scalar_prefetch=2,
            grid=(n_rows // tm, nf),
            in_specs=[
                pl.BlockSpec((tm, d), lambda i, f, te, tv: (i, 0)),
                pl.BlockSpec((None, None, d, tf), lambda i, f, te, tv: (layer, te[i], 0, wf(i, f, tv))),
                pl.BlockSpec((None, None, d, tf), lambda i, f, te, tv: (layer, te[i], 0, nf + wf(i, f, tv))),
                pl.BlockSpec((None, 1, tf), lambda i, f, te, tv: (te[i], 0, wf(i, f, tv))),
                pl.BlockSpec((None, 1, tf), lambda i, f, te, tv: (te[i], 0, nf + wf(i, f, tv))),
                pl.BlockSpec((None, None, tf, d), lambda i, f, te, tv: (layer, te[i], wf(i, f, tv), 0)),
                pl.BlockSpec((None, 1, d), lambda i, f, te, tv: (te[i], 0, 0)),
            ],
            out_specs=pl.BlockSpec((tm * PACK, LANES), lambda i, f, te, tv: (i, 0)),
            scratch_shapes=[pltpu.VMEM((tm, d), F32)],
        ),
        compiler_params=_params(("arbitrary", "arbitrary")),
        name="moe_experts",
    )(tile_expert, tile_valid, xs, wgu_bf, wgu_bf, b_gu.reshape(n_exp, 1, two_ff), b_gu.reshape(n_exp, 1, two_ff),
      wd_bf, b_down.reshape(n_exp, 1, d))


COMBINE_DEPTH = 2


def _combine_ln_kernel(*refs, tt, n_tiles):
    pos_refs = refs[:COMBINE_DEPTH]
    y_hbm, gate_ref, x_ref, g_ref, b_ref, o_ref = refs[COMBINE_DEPTH:COMBINE_DEPTH + 6]
    slabs = refs[COMBINE_DEPTH + 6:2 * COMBINE_DEPTH + 6]
    y_scr, sem = refs[2 * COMBINE_DEPTH + 6:]

    def weighted_sum(slab):
        gates = gate_ref[...]
        gk = [jnp.broadcast_to(gates[:, k:k + 1], (tt, LANES)) for k in range(TOPK)]
        for c in range(PACK):
            acc_lo = acc_hi = None
            for k in range(TOPK):
                lo, hi = _unpack_bf16_pair(slab[pl.ds(k * tt * PACK + c, tt, stride=PACK), :])
                acc_lo = gk[k] * lo if acc_lo is None else acc_lo + gk[k] * lo
                acc_hi = gk[k] * hi if acc_hi is None else acc_hi + gk[k] * hi
            y_scr[:, c * LANES:(c + 1) * LANES] = acc_lo
            y_scr[:, (c + PACK) * LANES:(c + PACK + 1) * LANES] = acc_hi

    _gather_ring(n_tiles, pos_refs, tt * TOPK, y_hbm, slabs, sem, weighted_sum, PACKED_ROW)
    o_ref[...] = _layer_norm_rows(DEEPNORM_ALPHA * x_ref[...] + y_scr[...], g_ref[...], b_ref[...])


def _combine_ln(pos, y_slab, gates, x2d, g, b):
    t, d = x2d.shape
    tt = 256
    n_tiles = t // tt
    pos3 = pos.reshape(n_tiles, tt, TOPK).transpose(0, 2, 1).reshape(n_tiles, 1, TOPK * tt)
    stage = pltpu.VMEM((tt * TOPK * PACKED_ROW.pitch, LANES), jnp.uint32)
    return pl.pallas_call(
        functools.partial(_combine_ln_kernel, tt=tt, n_tiles=n_tiles),
        out_shape=jax.ShapeDtypeStruct((t, d), F32),
        grid=(n_tiles,),
        in_specs=_ring_index_specs(n_tiles, tt * TOPK, COMBINE_DEPTH) + [
            pl.BlockSpec(memory_space=pl.ANY),
            pl.BlockSpec((tt, LANES), lambda i: (i, 0)),
            pl.BlockSpec((tt, d), lambda i: (i, 0)),
            pl.BlockSpec((1, d), lambda i: (0, 0)),
            pl.BlockSpec((1, d), lambda i: (0, 0)),
        ],
        out_specs=pl.BlockSpec((tt, d), lambda i: (i, 0)),
        scratch_shapes=[stage] * COMBINE_DEPTH + [pltpu.VMEM((tt, d), F32),
                                                  pltpu.SemaphoreType.DMA((COMBINE_DEPTH,))],
        compiler_params=_params(("arbitrary",)),
        name="moe_combine_ln",
    )(*([pos3] * COMBINE_DEPTH), y_slab, gates, x2d, g.reshape(1, d), b.reshape(1, d))


def _routing_tables(top_idx, n_experts, tm):
    t = top_idx.shape[0]
    n_assign = t * TOPK
    e_flat = top_idx.reshape(-1)
    onehot = (e_flat[:, None] == jnp.arange(n_experts, dtype=jnp.int32)[None, :]).astype(jnp.int32)
    csum = jnp.cumsum(onehot, axis=0)
    counts = csum[-1]
    rank = jnp.sum((csum - onehot) * onehot, axis=1)
    padded = (counts + tm - 1) // tm * tm
    p_end = jnp.cumsum(padded)
    p_start = p_end - padded
    start = jnp.cumsum(counts) - counts
    pos = (p_start[e_flat] + rank).astype(jnp.int32)
    n_tiles = (n_assign + n_experts * (tm - 1)) // tm
    n_rows = n_tiles * tm
    order = jnp.argsort(e_flat)
    tile_expert = jnp.minimum(jnp.searchsorted(p_end, jnp.arange(n_tiles) * tm, side="right"),
                              n_experts - 1).astype(jnp.int32)
    tile_valid = (jnp.arange(n_tiles) * tm < p_end[-1]).astype(jnp.int32)
    rows = jnp.arange(n_rows)
    row_e = jnp.repeat(tile_expert, tm)
    off = rows - p_start[row_e]
    real = (off < counts[row_e]) & (rows < p_end[-1])
    src = jnp.clip(start[row_e] + off, 0, n_assign - 1)
    row_tok = jnp.where(real, order[src] // TOPK, 0).astype(jnp.int32)
    tile_fill = jnp.clip(jnp.sum(real.reshape(n_tiles, tm), axis=1), 0, tm)
    chunk = _row_chunk(tm)
    tile_chunks = ((tile_fill + chunk - 1) // chunk).astype(jnp.int32)
    return row_tok, pos, tile_expert, tile_valid, tile_chunks, n_rows


MOE_TM = 512
MOE_TF = 1024


def kernel(x, mem, w_in, b_in, attn_sinks, w_out, ln1_g, ln1_b, w_mem_q, w_mem_k, w_mem_v, w_mem_o, ln2_g, ln2_b,
           w_router, b_router, w_gate_up, b_gate_up, w_down, b_down, ln3_g, ln3_b):
    bsz, s, d = x.shape
    t = bsz * s
    depth = w_in.shape[0]
    n_experts = w_router.shape[-1]
    mem2d = mem.reshape(bsz * mem.shape[1], d)
    x2d = x.reshape(t, d)
    wgu_bf = w_gate_up.astype(BF16)
    wd_bf = w_down.astype(BF16)
    tf = min(MOE_TF, w_down.shape[2])
    for i in range(depth):
        proj = _inproj(x2d, w_in[i].astype(BF16), b_in[i]).reshape(bsz, s, IN_WIDTH)
        oa = _swa_attention(proj, attn_sinks[i])
        ob = _moba_attention(proj)
        oc = _dilated_attention(proj)
        x1 = _outproj_ln(oa.reshape(t, A_Q_W), ob.reshape(t, B_W), oc.reshape(t, C_W), w_out[i].astype(BF16), x2d,
                         ln1_g[i], ln1_b[i])
        wkv = jnp.concatenate([w_mem_k[i], w_mem_v[i]], axis=1).astype(BF16)
        kv = _mem_kv(mem2d, wkv).reshape(bsz, mem.shape[1], wkv.shape[1])
        x2, x2_slab, top_idx, gates = _cross_ln_router(x1.reshape(bsz, s, d), kv, w_mem_q[i].astype(BF16),
                                                       w_mem_o[i].astype(BF16), ln2_g[i], ln2_b[i],
                                                       w_router[i], b_router[i])
        row_tok, pos, tile_expert, tile_valid, tile_chunks, n_rows = _routing_tables(top_idx[:, :TOPK], n_experts,
                                                                                     MOE_TM)
        xs = _dispatch_gather(row_tok, tile_chunks, x2_slab, n_rows, MOE_TM, d)
        y_slab = _expert_mlp(tile_expert, tile_valid, xs, wgu_bf, b_gate_up[i], wd_bf, b_down[i], i, MOE_TM, tf)
        x2d = _combine_ln(pos, y_slab, gates, x2, ln3_g[i], ln3_b[i])
    return x2d.reshape(bsz, s, d)
```
